```python
import jax, jax.numpy as jnp
from jax import lax
import numpy as np

D_MODEL = 4096
BATCH = 16
SEQ = 2048
DEPTH = 2

N_MIXERS = 2
EXPAND = 2
D_INNER = EXPAND * D_MODEL
CHUNK = 128
A_HEADS = 16
A_HEAD_DIM = D_INNER // A_HEADS
POOL_WINDOWS = (2, 4, 8, 16)
B_GROUPS = len(POOL_WINDOWS)
B_GROUP_DIM = D_INNER // B_GROUPS
N_A_LAYERS = (DEPTH + 1) // 2
N_B_LAYERS = DEPTH // 2
EPS = 1e-6

kernel_name = "hybrid_gmlp_pool_sandwich_trunk"


def rms_norm(x, g):
    x32 = x.astype(jnp.float32)
    y = x32 * lax.rsqrt(jnp.mean(x32 * x32, axis=-1, keepdims=True) + EPS)
    return (y * g.astype(jnp.float32)).astype(x.dtype)


def layer_norm(x, g, b):
    x32 = x.astype(jnp.float32)
    mu = jnp.mean(x32, axis=-1, keepdims=True)
    xc = x32 - mu
    y = xc * lax.rsqrt(jnp.mean(xc * xc, axis=-1, keepdims=True) + EPS)
    return (y * g.astype(jnp.float32) + b.astype(jnp.float32)).astype(x.dtype)


def mixer_a(h, w_in, ln_g, ln_b, w_s, b_s, w_out):
    bsz, s, _ = h.shape
    proj = h @ w_in
    u, v, z = jnp.split(proj, 3, axis=-1)
    u = jax.nn.gelu(u, approximate=False)
    v = layer_norm(jax.nn.gelu(v, approximate=False), ln_g, ln_b)
    v = v.reshape(bsz, s // CHUNK, CHUNK, A_HEADS, A_HEAD_DIM)
    w_causal = jnp.tril(w_s)
    sv = jnp.einsum('hts,bnshd->bnthd', w_causal, v) + b_s.T[:, :, None]
    sgu = u * sv.reshape(bsz, s, D_INNER)
    y = sgu * jax.nn.silu(z)
    return y @ w_out


def mixer_b(h, w_in, w_grp, b_grp, scale, w_out):
    bsz, s, _ = h.shape
    proj = h @ w_in
    p, z = jnp.split(proj, 2, axis=-1)
    p = p.reshape(bsz, s, B_GROUPS, B_GROUP_DIM)
    p32 = p.astype(jnp.float32)
    csum = jnp.cumsum(p32, axis=1)
    pos = jnp.arange(1, s + 1, dtype=jnp.float32)
    means = []
    for k, w in enumerate(POOL_WINDOWS):
        ck = csum[:, :, k]
        lagged = jnp.pad(ck, ((0, 0), (w, 0), (0, 0)))[:, :s]
        count = jnp.minimum(pos, float(w))[:, None]
        means.append((ck - lagged) / count)
    pooled = (jnp.stack(means, axis=2) - p32).astype(p.dtype)
    mixed = jnp.einsum('bsgc,gcd->bsgd', pooled, w_grp) + b_grp
    mixed = mixed.reshape(bsz, s, D_INNER) * scale
    y = mixed * jax.nn.silu(z)
    return y @ w_out


def _fwd_setup_inputs(seed: int = 0) -> dict:
    key = jax.random.key(seed)
    ks = jax.random.split(key, 16)
    f32 = jnp.float32
    nrm = lambda k, shape, sc: jax.random.normal(k, shape, f32) * sc
    x = nrm(ks[0], (BATCH, SEQ, D_MODEL), 1.0)
    pre_norm = 1.0 + nrm(ks[1], (DEPTH, D_MODEL), 0.02)
    post_norm = 1.0 + nrm(ks[2], (DEPTH, D_MODEL), 0.02)
    a_w_in = nrm(ks[3], (N_A_LAYERS, D_MODEL, 3 * D_INNER), D_MODEL ** -0.5)
    a_ln_g = 1.0 + nrm(ks[4], (N_A_LAYERS, D_INNER), 0.02)
    a_ln_b = nrm(ks[5], (N_A_LAYERS, D_INNER), 0.02)
    a_w_s = nrm(ks[6], (N_A_LAYERS, A_HEADS, CHUNK, CHUNK), CHUNK ** -0.5)
    a_b_s = 1.0 + nrm(ks[7], (N_A_LAYERS, A_HEADS, CHUNK), 0.01)
    a_w_out = nrm(ks[8], (N_A_LAYERS, D_INNER, D_MODEL), D_INNER ** -0.5)
    b_w_in = nrm(ks[9], (N_B_LAYERS, D_MODEL, 2 * D_INNER), D_MODEL ** -0.5)
    b_w_grp = nrm(ks[10], (N_B_LAYERS, B_GROUPS, B_GROUP_DIM, B_GROUP_DIM), B_GROUP_DIM ** -0.5)
    b_b_grp = nrm(ks[11], (N_B_LAYERS, B_GROUPS, B_GROUP_DIM), 0.02)
    b_scale = 1.0 + nrm(ks[12], (N_B_LAYERS, D_INNER), 0.02)
    b_w_out = nrm(ks[13], (N_B_LAYERS, D_INNER, D_MODEL), D_INNER ** -0.5)
    return {"x": x, "pre_norm": pre_norm, "post_norm": post_norm,
            "a_w_in": a_w_in, "a_ln_g": a_ln_g, "a_ln_b": a_ln_b,
            "a_w_s": a_w_s, "a_b_s": a_b_s, "a_w_out": a_w_out,
            "b_w_in": b_w_in, "b_w_grp": b_w_grp, "b_b_grp": b_b_grp,
            "b_scale": b_scale, "b_w_out": b_w_out}


def _fwd_reference(x, pre_norm, post_norm, a_w_in, a_ln_g, a_ln_b, a_w_s, a_b_s, a_w_out,
              b_w_in, b_w_grp, b_b_grp, b_scale, b_w_out):
    for i in range(DEPTH):
        h = rms_norm(x, pre_norm[i])
        j = i // N_MIXERS
        if i % N_MIXERS == 0:
            m = mixer_a(h, a_w_in[j], a_ln_g[j], a_ln_b[j], a_w_s[j], a_b_s[j], a_w_out[j])
        else:
            m = mixer_b(h, b_w_in[j], b_w_grp[j], b_b_grp[j], b_scale[j], b_w_out[j])
        x = x + rms_norm(m, post_norm[i])
    return x


import jax as _jax
import jax.numpy as _jnp

TWIN_FORMAT = 'train_step'
FWD_PARAMS = ['x', 'pre_norm', 'post_norm', 'a_w_in', 'a_ln_g', 'a_ln_b', 'a_w_s', 'a_b_s', 'a_w_out', 'b_w_in', 'b_w_grp', 'b_b_grp', 'b_scale', 'b_w_out']
TWIN_WEIGHTS = ['pre_norm', 'post_norm', 'a_w_in', 'a_ln_g', 'a_ln_b', 'a_w_s', 'a_b_s', 'a_w_out', 'b_w_in', 'b_w_grp', 'b_b_grp', 'b_scale', 'b_w_out']
TWIN_DIFF_INPUT = 'x'
TWIN_INPUTS = ['x', 'pre_norm', 'post_norm', 'a_w_in', 'a_ln_g', 'a_ln_b', 'a_w_s', 'a_b_s', 'a_w_out', 'b_w_in', 'b_w_grp', 'b_b_grp', 'b_scale', 'b_w_out', 'loss_target', 'm_pre_norm', 'm_post_norm', 'm_a_w_in', 'm_a_ln_g', 'm_a_ln_b', 'm_a_w_s', 'm_a_b_s', 'm_a_w_out', 'm_b_w_in', 'm_b_w_grp', 'm_b_b_grp', 'm_b_scale', 'm_b_w_out', 'v_pre_norm', 'v_post_norm', 'v_a_w_in', 'v_a_ln_g', 'v_a_ln_b', 'v_a_w_s', 'v_a_b_s', 'v_a_w_out', 'v_b_w_in', 'v_b_w_grp', 'v_b_b_grp', 'v_b_scale', 'v_b_w_out']
TWIN_OUTPUTS = ['loss', 'grad_x', 'grad_pre_norm', 'grad_post_norm', 'grad_a_w_in', 'grad_a_ln_g', 'grad_a_ln_b', 'grad_a_w_s', 'grad_a_b_s', 'grad_a_w_out', 'grad_b_w_in', 'grad_b_w_grp', 'grad_b_b_grp', 'grad_b_scale', 'grad_b_w_out', 'delta_pre_norm', 'delta_post_norm', 'delta_a_w_in', 'delta_a_ln_g', 'delta_a_ln_b', 'delta_a_w_s', 'delta_a_b_s', 'delta_a_w_out', 'delta_b_w_in', 'delta_b_w_grp', 'delta_b_b_grp', 'delta_b_scale', 'delta_b_w_out', 'new_m_pre_norm', 'new_m_post_norm', 'new_m_a_w_in', 'new_m_a_ln_g', 'new_m_a_ln_b', 'new_m_a_w_s', 'new_m_a_b_s', 'new_m_a_w_out', 'new_m_b_w_in', 'new_m_b_w_grp', 'new_m_b_b_grp', 'new_m_b_scale', 'new_m_b_w_out', 'new_v_pre_norm', 'new_v_post_norm', 'new_v_a_w_in', 'new_v_a_ln_g', 'new_v_a_ln_b', 'new_v_a_w_s', 'new_v_a_b_s', 'new_v_a_w_out', 'new_v_b_w_in', 'new_v_b_w_grp', 'new_v_b_b_grp', 'new_v_b_scale', 'new_v_b_w_out']
TWIN_LEAF_KINDS = {'loss': 'loss', 'grad_x': 'grad_x', 'grad_pre_norm': 'grad_w', 'grad_post_norm': 'grad_w', 'grad_a_w_in': 'grad_w', 'grad_a_ln_g': 'grad_w', 'grad_a_ln_b': 'grad_w', 'grad_a_w_s': 'grad_w', 'grad_a_b_s': 'grad_w', 'grad_a_w_out': 'grad_w', 'grad_b_w_in': 'grad_w', 'grad_b_w_grp': 'grad_w', 'grad_b_b_grp': 'grad_w', 'grad_b_scale': 'grad_w', 'grad_b_w_out': 'grad_w', 'delta_pre_norm': 'delta_w', 'delta_post_norm': 'delta_w', 'delta_a_w_in': 'delta_w', 'delta_a_ln_g': 'delta_w', 'delta_a_ln_b': 'delta_w', 'delta_a_w_s': 'delta_w', 'delta_a_b_s': 'delta_w', 'delta_a_w_out': 'delta_w', 'delta_b_w_in': 'delta_w', 'delta_b_w_grp': 'delta_w', 'delta_b_b_grp': 'delta_w', 'delta_b_scale': 'delta_w', 'delta_b_w_out': 'delta_w', 'new_m_pre_norm': 'new_m', 'new_m_post_norm': 'new_m', 'new_m_a_w_in': 'new_m', 'new_m_a_ln_g': 'new_m', 'new_m_a_ln_b': 'new_m', 'new_m_a_w_s': 'new_m', 'new_m_a_b_s': 'new_m', 'new_m_a_w_out': 'new_m', 'new_m_b_w_in': 'new_m', 'new_m_b_w_grp': 'new_m', 'new_m_b_b_grp': 'new_m', 'new_m_b_scale': 'new_m', 'new_m_b_w_out': 'new_m', 'new_v_pre_norm': 'new_v', 'new_v_post_norm': 'new_v', 'new_v_a_w_in': 'new_v', 'new_v_a_ln_g': 'new_v', 'new_v_a_ln_b': 'new_v', 'new_v_a_w_s': 'new_v', 'new_v_a_b_s': 'new_v', 'new_v_a_w_out': 'new_v', 'new_v_b_w_in': 'new_v', 'new_v_b_w_grp': 'new_v', 'new_v_b_b_grp': 'new_v', 'new_v_b_scale': 'new_v', 'new_v_b_w_out': 'new_v'}


def _forward(args):
    return _fwd_reference(*[args[k] for k in FWD_PARAMS])


def _output_shape():
    def fwd():
        inp = _fwd_setup_inputs(0)
        return _fwd_reference(*[inp[k] for k in FWD_PARAMS])
    out = _jax.eval_shape(fwd)
    return out.shape, out.dtype

N_MICROBATCH = 1
ADAM_LR = 0.001
ADAM_B1 = 0.9
ADAM_B2 = 0.999
ADAM_EPS = 1e-08
ADAM_WD = 0.01
ADAM_STEP = 10
PER_EXAMPLE_BATCH_AXIS = {'x': 0, 'loss_target': 0}
SHARED_INPUTS = []
_WEIGHT_DTYPES = {'pre_norm': _jnp.float32, 'post_norm': _jnp.float32, 'a_w_in': _jnp.float32, 'a_ln_g': _jnp.float32, 'a_ln_b': _jnp.float32, 'a_w_s': _jnp.float32, 'a_b_s': _jnp.float32, 'a_w_out': _jnp.float32, 'b_w_in': _jnp.float32, 'b_w_grp': _jnp.float32, 'b_b_grp': _jnp.float32, 'b_scale': _jnp.float32, 'b_w_out': _jnp.float32}
MOMENT_SCALE = {'pre_norm': 1.496756e-01, 'post_norm': 7.994510e+00, 'a_w_in': 7.250793e-02, 'a_ln_g': 4.391875e-02, 'a_ln_b': 4.261255e-02, 'a_w_s': 8.612835e-02, 'a_b_s': 1.270096e-01, 'a_w_out': 1.163928e-01, 'b_w_in': 5.920537e-02, 'b_w_grp': 6.185334e-02, 'b_b_grp': 2.048567e-01, 'b_scale': 6.118107e-02, 'b_w_out': 8.771039e-02}


def _to_microbatches(a, axis):
    t = _jnp.moveaxis(a, axis, 0)
    t = t.reshape((N_MICROBATCH, t.shape[0] // N_MICROBATCH) + t.shape[1:])
    return _jnp.moveaxis(t, 1, axis + 1)


def setup_inputs(seed: int = 0) -> dict:
    inp = _fwd_setup_inputs(seed)
    key = _jax.random.fold_in(_jax.random.key(seed), 7919)
    shape, _ = _output_shape()
    out = dict(inp)
    out["loss_target"] = _jax.random.normal(_jax.random.fold_in(key, 0), shape, _jnp.float32)
    for i, name in enumerate(TWIN_WEIGHTS):
        w = inp[name].astype(_jnp.float32)
        if MOMENT_SCALE is None:
            s = _jnp.sqrt(_jnp.mean(_jnp.square(w)) + 1e-30)
        else:
            s = MOMENT_SCALE[name]
        km, kv = _jax.random.split(_jax.random.fold_in(key, i + 1))
        out[name] = w
        out["m_" + name] = s * _jax.random.normal(km, w.shape, _jnp.float32)
        out["v_" + name] = (s * s) * _jax.random.uniform(kv, w.shape, _jnp.float32, 0.5, 1.5)
    if N_MICROBATCH > 1:
        for name, axis in PER_EXAMPLE_BATCH_AXIS.items():
            out[name] = _to_microbatches(out[name], axis)
    return {'x': out['x'], 'pre_norm': out['pre_norm'], 'post_norm': out['post_norm'], 'a_w_in': out['a_w_in'], 'a_ln_g': out['a_ln_g'], 'a_ln_b': out['a_ln_b'], 'a_w_s': out['a_w_s'], 'a_b_s': out['a_b_s'], 'a_w_out': out['a_w_out'], 'b_w_in': out['b_w_in'], 'b_w_grp': out['b_w_grp'], 'b_b_grp': out['b_b_grp'], 'b_scale': out['b_scale'], 'b_w_out': out['b_w_out'], 'loss_target': out['loss_target'], 'm_pre_norm': out['m_pre_norm'], 'm_post_norm': out['m_post_norm'], 'm_a_w_in': out['m_a_w_in'], 'm_a_ln_g': out['m_a_ln_g'], 'm_a_ln_b': out['m_a_ln_b'], 'm_a_w_s': out['m_a_w_s'], 'm_a_b_s': out['m_a_b_s'], 'm_a_w_out': out['m_a_w_out'], 'm_b_w_in': out['m_b_w_in'], 'm_b_w_grp': out['m_b_w_grp'], 'm_b_b_grp': out['m_b_b_grp'], 'm_b_scale': out['m_b_scale'], 'm_b_w_out': out['m_b_w_out'], 'v_pre_norm': out['v_pre_norm'], 'v_post_norm': out['v_post_norm'], 'v_a_w_in': out['v_a_w_in'], 'v_a_ln_g': out['v_a_ln_g'], 'v_a_ln_b': out['v_a_ln_b'], 'v_a_w_s': out['v_a_w_s'], 'v_a_b_s': out['v_a_b_s'], 'v_a_w_out': out['v_a_w_out'], 'v_b_w_in': out['v_b_w_in'], 'v_b_w_grp': out['v_b_w_grp'], 'v_b_b_grp': out['v_b_b_grp'], 'v_b_scale': out['v_b_scale'], 'v_b_w_out': out['v_b_w_out']}


def _loss(weights, diff, rest, loss_target):
    with _jax.named_scope("forward"):
        args = {**rest, TWIN_DIFF_INPUT: diff, **{k: w.astype(_WEIGHT_DTYPES[k]) for k, w in weights.items()}}
        y = _forward(args)
    with _jax.named_scope("loss_head"):
        err = _jnp.square(y.astype(_jnp.float32) - loss_target)
        return 0.5 * _jnp.sum(_jnp.mean(err, axis=-1)) if err.ndim else 0.5 * err


def _adamw(w, g, m, v):
    m = ADAM_B1 * m + (1.0 - ADAM_B1) * g
    v = ADAM_B2 * v + (1.0 - ADAM_B2) * _jnp.square(g)
    m_hat = m / (1.0 - ADAM_B1 ** ADAM_STEP)
    v_hat = v / (1.0 - ADAM_B2 ** ADAM_STEP)
    delta = -ADAM_LR * (m_hat / (_jnp.sqrt(v_hat) + ADAM_EPS) + ADAM_WD * w)
    return delta, m, v


def reference(x, pre_norm, post_norm, a_w_in, a_ln_g, a_ln_b, a_w_s, a_b_s, a_w_out, b_w_in, b_w_grp, b_b_grp, b_scale, b_w_out, loss_target, m_pre_norm, m_post_norm, m_a_w_in, m_a_ln_g, m_a_ln_b, m_a_w_s, m_a_b_s, m_a_w_out, m_b_w_in, m_b_w_grp, m_b_b_grp, m_b_scale, m_b_w_out, v_pre_norm, v_post_norm, v_a_w_in, v_a_ln_g, v_a_ln_b, v_a_w_s, v_a_b_s, v_a_w_out, v_b_w_in, v_b_w_grp, v_b_b_grp, v_b_scale, v_b_w_out):
    given = dict(x=x, pre_norm=pre_norm, post_norm=post_norm, a_w_in=a_w_in, a_ln_g=a_ln_g, a_ln_b=a_ln_b, a_w_s=a_w_s, a_b_s=a_b_s, a_w_out=a_w_out, b_w_in=b_w_in, b_w_grp=b_w_grp, b_b_grp=b_b_grp, b_scale=b_scale, b_w_out=b_w_out, loss_target=loss_target, m_pre_norm=m_pre_norm, m_post_norm=m_post_norm, m_a_w_in=m_a_w_in, m_a_ln_g=m_a_ln_g, m_a_ln_b=m_a_ln_b, m_a_w_s=m_a_w_s, m_a_b_s=m_a_b_s, m_a_w_out=m_a_w_out, m_b_w_in=m_b_w_in, m_b_w_grp=m_b_w_grp, m_b_b_grp=m_b_b_grp, m_b_scale=m_b_scale, m_b_w_out=m_b_w_out, v_pre_norm=v_pre_norm, v_post_norm=v_post_norm, v_a_w_in=v_a_w_in, v_a_ln_g=v_a_ln_g, v_a_ln_b=v_a_ln_b, v_a_w_s=v_a_w_s, v_a_b_s=v_a_b_s, v_a_w_out=v_a_w_out, v_b_w_in=v_b_w_in, v_b_w_grp=v_b_w_grp, v_b_b_grp=v_b_b_grp, v_b_scale=v_b_scale, v_b_w_out=v_b_w_out)
    weights = {n: given[n] for n in TWIN_WEIGHTS}
    shared = {n: given[n] for n in SHARED_INPUTS}
    per_example = {n: given[n] for n in ['x']}
    grad_fn = _jax.value_and_grad(_loss, argnums=(0, 1))

    def one_microbatch(ex, loss_target):
        ex = dict(ex)
        diff = ex.pop(TWIN_DIFF_INPUT)
        return grad_fn(weights, diff, {**shared, **ex}, loss_target)

    if N_MICROBATCH == 1:
        loss, (grad_w, grad_x) = one_microbatch(per_example, given["loss_target"])
    else:
        def body(carry, xs):
            loss_sum, grad_sum = carry
            l_k, (gw_k, gx_k) = one_microbatch(xs[0], xs[1])
            with _jax.named_scope("update"):
                return (loss_sum + l_k, _jax.tree.map(_jnp.add, grad_sum, gw_k)), gx_k

        init = (_jnp.zeros((), _jnp.float32), _jax.tree.map(_jnp.zeros_like, weights))
        (loss, grad_w), grad_x = _jax.lax.scan(body, init, (per_example, given["loss_target"]))
    with _jax.named_scope("update"):
        delta_w, new_m, new_v = {}, {}, {}
        for n in TWIN_WEIGHTS:
            delta_w[n], new_m[n], new_v[n] = _adamw(weights[n], grad_w[n], given["m_" + n], given["v_" + n])
    return (loss, grad_x, *[grad_w[n] for n in TWIN_WEIGHTS], *[delta_w[n] for n in TWIN_WEIGHTS],
            *[new_m[n] for n in TWIN_WEIGHTS], *[new_v[n] for n in TWIN_WEIGHTS])
```

```python
import functools
import math

import jax
import jax.numpy as jnp
from jax import lax
from jax.experimental import pallas as pl
from jax.experimental.pallas import tpu as pltpu

NDEV = 8
CHUNK = 128
HEADS = 16
WINDOWS = (2, 4, 8, 16)
HALO = 16
NORM_EPS = 1e-6
ADAM_LR, ADAM_B1, ADAM_B2, ADAM_EPS, ADAM_WD, ADAM_STEP = 0.001, 0.9, 0.999, 1e-08, 0.01, 10
LANES = 128
F32_SUBLANES = 8
VMEM_LIMIT = 60 * 1024 * 1024
BF16 = jnp.bfloat16
F32 = jnp.float32
MESH_ID = pl.DeviceIdType.MESH
ANY = pl.BlockSpec(memory_space=pl.ANY)


def _tile(dim, pref, mult=LANES):
    t = (min(pref, dim) // mult) * mult
    while t >= mult:
        if dim % t == 0:
            return t
        t -= mult
    return dim


def _params(*sem):
    return pltpu.CompilerParams(dimension_semantics=sem, vmem_limit_bytes=VMEM_LIMIT)


def _gelu(x):
    return 0.5 * x * (1.0 + lax.erf(x * (1.0 / math.sqrt(2.0))))


def _gelu_grad(x):
    return 0.5 * (1.0 + lax.erf(x * (1.0 / math.sqrt(2.0)))) + x * jnp.exp(-0.5 * x * x) * (1.0 / math.sqrt(2.0 * math.pi))


_DOT_DIMS = {"nn": (((1,), (0,)), ((), ())), "nt": (((1,), (1,)), ((), ())), "tn": (((0,), (0,)), ((), ()))}


def _matmul(name, a, b, *, mode, grid, a_spec, b_spec, o_spec, out_shape, nk):
    dims = _DOT_DIMS[mode]
    acc_shape = tuple(d for d in o_spec.block_shape if d is not None)

    if nk == 1:
        def body(a_ref, b_ref, o_ref):
            o_ref[...] = lax.dot_general(a_ref[...], b_ref[...], dims, preferred_element_type=F32).astype(o_ref.dtype)
        scratch = []
        sem = ("parallel",) * len(grid)
    else:
        def body(a_ref, b_ref, o_ref, acc_ref):
            k = pl.program_id(len(grid) - 1)
            part = lax.dot_general(a_ref[...], b_ref[...], dims, preferred_element_type=F32)

            @pl.when(k == 0)
            def _():
                acc_ref[...] = part

            @pl.when(jnp.logical_and(k > 0, k < nk - 1))
            def _():
                acc_ref[...] += part

            @pl.when(k == nk - 1)
            def _():
                o_ref[...] = (acc_ref[...] + part).astype(o_ref.dtype)
        scratch = [pltpu.VMEM(acc_shape, F32)]
        sem = ("parallel",) * (len(grid) - 1) + ("arbitrary",)

    return pl.pallas_call(
        body, name=name, grid=grid, in_specs=[a_spec, b_spec], out_specs=o_spec, out_shape=out_shape,
        scratch_shapes=scratch, compiler_params=_params(*sem))(a, b)


def _mm_in_proj(name, h, w_g, bm, bn):
    t, d = h.shape
    _, _, c = w_g.shape
    bn = _tile(c, bn)
    nb = c // bn
    return _matmul(
        name, h, w_g, mode="nn", grid=(t // bm, NDEV * nb), nk=1,
        a_spec=pl.BlockSpec((bm, d), lambda i, j: (i, 0)),
        b_spec=pl.BlockSpec((None, d, bn), lambda i, j: (j // nb, 0, j % nb)),
        o_spec=pl.BlockSpec((bm, bn), lambda i, j: (i, j)),
        out_shape=jax.ShapeDtypeStruct((t, NDEV * c), BF16))


def _mm_out_proj(name, y, w, bm, bn, bk):
    t, e = y.shape
    _, d = w.shape
    nk = e // bk
    return _matmul(
        name, y, w, mode="nn", grid=(t // bm, d // bn, nk), nk=nk,
        a_spec=pl.BlockSpec((bm, bk), lambda i, j, k: (i, k)),
        b_spec=pl.BlockSpec((bk, bn), lambda i, j, k: (k, j)),
        o_spec=pl.BlockSpec((bm, bn), lambda i, j, k: (i, j)),
        out_shape=jax.ShapeDtypeStruct((t, d), F32))


def _mm_grouped(name, a, w, mode, bm, bn, out_dtype):
    t, e = a.shape
    g_n, gw, _ = w.shape
    nb = gw // bn
    if mode == "nn":
        b_spec = pl.BlockSpec((None, gw, bn), lambda g, i, j: (g, 0, j))
    else:
        b_spec = pl.BlockSpec((None, bn, gw), lambda g, i, j: (g, j, 0))
    return _matmul(
        name, a, w, mode=mode, grid=(g_n, t // bm, nb), nk=1,
        a_spec=pl.BlockSpec((bm, gw), lambda g, i, j: (i, g)),
        b_spec=b_spec,
        o_spec=pl.BlockSpec((bm, bn), lambda g, i, j: (i, g * nb + j)),
        out_shape=jax.ShapeDtypeStruct((t, e), out_dtype))


def _mm_dact_out(name, dm, w, bm, bn):
    t, d = dm.shape
    e, _ = w.shape
    return _matmul(
        name, dm, w, mode="nt", grid=(t // bm, e // bn), nk=1,
        a_spec=pl.BlockSpec((bm, d), lambda i, j: (i, 0)),
        b_spec=pl.BlockSpec((bn, d), lambda i, j: (j, 0)),
        o_spec=pl.BlockSpec((bm, bn), lambda i, j: (i, j)),
        out_shape=jax.ShapeDtypeStruct((t, e), BF16))


def _mm_dact_in(name, dproj, w_g, bm, bn):
    t, _ = dproj.shape
    _, d, c = w_g.shape
    return _matmul(
        name, dproj, w_g, mode="nt", grid=(t // bm, d // bn, NDEV), nk=NDEV,
        a_spec=pl.BlockSpec((bm, c), lambda i, j, k: (i, k)),
        b_spec=pl.BlockSpec((None, bn, c), lambda i, j, k: (k, j, 0)),
        o_spec=pl.BlockSpec((bm, bn), lambda i, j, k: (i, j)),
        out_shape=jax.ShapeDtypeStruct((t, d), F32))


def _mm_dw(name, a, b, bm, bn):
    t, m = a.shape
    _, n = b.shape
    return _matmul(
        name, a, b, mode="tn", grid=(m // bm, n // bn), nk=1,
        a_spec=pl.BlockSpec((t, bm), lambda i, j: (0, i)),
        b_spec=pl.BlockSpec((t, bn), lambda i, j: (0, j)),
        o_spec=pl.BlockSpec((bm, bn), lambda i, j: (i, j)),
        out_shape=jax.ShapeDtypeStruct((m, n), BF16))


def _mm_dw_sharded(name, a, b, c, bm, bn):
    t, m = a.shape
    bn = _tile(c, bn)
    nb = c // bn
    return _matmul(
        name, a, b, mode="tn", grid=(m // bm, NDEV * nb), nk=1,
        a_spec=pl.BlockSpec((t, bm), lambda i, j: (0, i)),
        b_spec=pl.BlockSpec((t, bn), lambda i, j: (0, j)),
        o_spec=pl.BlockSpec((None, bm, bn), lambda i, j: (j // nb, i, j % nb)),
        out_shape=jax.ShapeDtypeStruct((NDEV, m, c), BF16))


def _mm_dw_grouped(name, a, b, g_n, bm, bn):
    t, e = a.shape
    gw = e // g_n
    nbm, nbn = gw // bm, gw // bn
    return _matmul(
        name, a, b, mode="tn", grid=(g_n, nbm, nbn), nk=1,
        a_spec=pl.BlockSpec((t, bm), lambda g, i, j: (0, g * nbm + i)),
        b_spec=pl.BlockSpec((t, bn), lambda g, i, j: (0, g * nbn + j)),
        o_spec=pl.BlockSpec((None, bm, bn), lambda g, i, j: (g, i, j)),
        out_shape=jax.ShapeDtypeStruct((g_n, gw, gw), BF16))


def _rms(x):
    return lax.rsqrt(jnp.mean(x * x, axis=-1, keepdims=True) + NORM_EPS)


def _rms_bwd(dy, x, g):
    r = _rms(x)
    xhat = x * r
    dxhat = dy * g
    dx = r * (dxhat - xhat * jnp.mean(dxhat * xhat, axis=-1, keepdims=True))
    return dx, dy * xhat


def _rms_fwd(name, x, g, tr):
    t, d = x.shape

    def body(x_ref, g_ref, h_ref):
        xv = x_ref[...]
        h_ref[...] = (xv * _rms(xv) * g_ref[...]).astype(BF16)

    row = pl.BlockSpec((tr, d), lambda i: (i, 0))
    vec = pl.BlockSpec((1, d), lambda i: (0, 0))
    return pl.pallas_call(
        body, name=name, grid=(t // tr,), in_specs=[row, vec], out_specs=row,
        out_shape=jax.ShapeDtypeStruct((t, d), BF16), compiler_params=_params("parallel"))(x, g)


def _post_pre(name, x, m, g_post, g_pre, tr):
    t, d = x.shape

    def body(x_ref, m_ref, gp_ref, gn_ref, x1_ref, h_ref):
        mv = m_ref[...]
        x1 = x_ref[...] + mv * _rms(mv) * gp_ref[...]
        x1_ref[...] = x1
        h_ref[...] = (x1 * _rms(x1) * gn_ref[...]).astype(BF16)

    row = pl.BlockSpec((tr, d), lambda i: (i, 0))
    vec = pl.BlockSpec((1, d), lambda i: (0, 0))
    return pl.pallas_call(
        body, name=name, grid=(t // tr,), in_specs=[row, row, vec, vec], out_specs=[row, row],
        out_shape=[jax.ShapeDtypeStruct((t, d), F32), jax.ShapeDtypeStruct((t, d), BF16)],
        compiler_params=_params("parallel"))(x, m, g_post, g_pre)


def _loss_head(name, x1, m, tgt, g_post, tr):
    t, d = x1.shape

    def body(x1_ref, m_ref, t_ref, g_ref, sq_ref, dres_ref, dm_ref, dg_ref):
        mv = m_ref[...]
        g = g_ref[...]
        diff = x1_ref[...] + mv * _rms(mv) * g - t_ref[...]
        dout = diff * (1.0 / d)
        dm, dg_rows = _rms_bwd(dout, mv, g)
        dres_ref[...] = dout
        dm_ref[...] = dm.astype(BF16)

        @pl.when(pl.program_id(0) == 0)
        def _():
            sq_ref[...] = jnp.zeros_like(sq_ref)
            dg_ref[...] = jnp.zeros_like(dg_ref)

        sq_ref[...] += jnp.sum(diff * diff, axis=0, keepdims=True)
        dg_ref[...] += jnp.sum(dg_rows, axis=0, keepdims=True)

    row = pl.BlockSpec((tr, d), lambda i: (i, 0))
    vec = pl.BlockSpec((1, d), lambda i: (0, 0))
    return pl.pallas_call(
        body, name=name, grid=(t // tr,), in_specs=[row, row, row, vec], out_specs=[vec, row, row, vec],
        out_shape=[jax.ShapeDtypeStruct((1, d), F32), jax.ShapeDtypeStruct((t, d), F32),
                   jax.ShapeDtypeStruct((t, d), BF16), jax.ShapeDtypeStruct((1, d), F32)],
        compiler_params=_params("arbitrary"))(x1, m, tgt, g_post)


def _norms_bwd_mid(name, dres, dh, x1, g_pre, m, g_post, tr):
    t, d = x1.shape

    def body(dres_ref, dh_ref, x1_ref, gpre_ref, m_ref, gpost_ref, out_ref, dm_ref, dgpre_ref, dgpost_ref):
        dx, dgpre_rows = _rms_bwd(dh_ref[...], x1_ref[...], gpre_ref[...])
        dres2 = dres_ref[...] + dx
        out_ref[...] = dres2
        dm, dgpost_rows = _rms_bwd(dres2, m_ref[...], gpost_ref[...])
        dm_ref[...] = dm.astype(BF16)

        @pl.when(pl.program_id(0) == 0)
        def _():
            dgpre_ref[...] = jnp.zeros_like(dgpre_ref)
            dgpost_ref[...] = jnp.zeros_like(dgpost_ref)

        dgpre_ref[...] += jnp.sum(dgpre_rows, axis=0, keepdims=True)
        dgpost_ref[...] += jnp.sum(dgpost_rows, axis=0, keepdims=True)

    row = pl.BlockSpec((tr, d), lambda i: (i, 0))
    vec = pl.BlockSpec((1, d), lambda i: (0, 0))
    return pl.pallas_call(
        body, name=name, grid=(t // tr,), in_specs=[row, row, row, vec, row, vec], out_specs=[row, row, vec, vec],
        out_shape=[jax.ShapeDtypeStruct((t, d), F32), jax.ShapeDtypeStruct((t, d), BF16),
                   jax.ShapeDtypeStruct((1, d), F32), jax.ShapeDtypeStruct((1, d), F32)],
        compiler_params=_params("arbitrary"))(dres, dh, x1, g_pre, m, g_post)


def _norm_bwd_first(name, dres, dh, x, g_pre, tr):
    t, d = x.shape

    def body(dres_ref, dh_ref, x_ref, g_ref, out_ref, dg_ref):
        dx, dg_rows = _rms_bwd(dh_ref[...], x_ref[...], g_ref[...])
        out_ref[...] = dres_ref[...] + dx

        @pl.when(pl.program_id(0) == 0)
        def _():
            dg_ref[...] = jnp.zeros_like(dg_ref)

        dg_ref[...] += jnp.sum(dg_rows, axis=0, keepdims=True)

    row = pl.BlockSpec((tr, d), lambda i: (i, 0))
    vec = pl.BlockSpec((1, d), lambda i: (0, 0))
    return pl.pallas_call(
        body, name=name, grid=(t // tr,), in_specs=[row, row, row, vec], out_specs=[row, vec],
        out_shape=[jax.ShapeDtypeStruct((t, d), F32), jax.ShapeDtypeStruct((1, d), F32)],
        compiler_params=_params("arbitrary"))(dres, dh, x, g_pre)


def _gate_a_fwd(name, proj, ln_g, ln_b, w_tril, bias_t):
    t, e3 = proj.shape
    e = e3 // 3
    hd = e // HEADS

    def body(p_ref, g_ref, b_ref, w_ref, bs_ref, y_ref, vg_ref):
        s1 = jnp.zeros((CHUNK, 1), F32)
        s2 = jnp.zeros((CHUNK, 1), F32)
        for h in range(HEADS):
            cols = slice(h * hd, (h + 1) * hd)
            vg = _gelu(p_ref[:, e + h * hd:e + (h + 1) * hd].astype(F32))
            vg_ref[:, cols] = vg
            s1 += jnp.sum(vg, axis=1, keepdims=True)
            s2 += jnp.sum(vg * vg, axis=1, keepdims=True)
        mu = s1 * (1.0 / e)
        rstd = lax.rsqrt(s2 * (1.0 / e) - mu * mu + NORM_EPS)
        for h in range(HEADS):
            cols = slice(h * hd, (h + 1) * hd)
            vn = ((vg_ref[:, cols] - mu) * rstd * g_ref[:, cols] + b_ref[:, cols]).astype(BF16)
            sv = jnp.dot(w_ref[h], vn, preferred_element_type=F32) + bs_ref[h]
            u = _gelu(p_ref[:, cols].astype(F32))
            z = p_ref[:, 2 * e + h * hd:2 * e + (h + 1) * hd].astype(F32)
            y_ref[:, cols] = (u * sv * (z * jax.nn.sigmoid(z))).astype(BF16)

    vec = pl.BlockSpec((1, e), lambda i: (0, 0))
    return pl.pallas_call(
        body, name=name, grid=(t // CHUNK,),
        in_specs=[pl.BlockSpec((CHUNK, e3), lambda i: (i, 0)), vec, vec,
                  pl.BlockSpec((HEADS, CHUNK, CHUNK), lambda i: (0, 0, 0)),
                  pl.BlockSpec((HEADS, CHUNK, 1), lambda i: (0, 0, 0))],
        out_specs=pl.BlockSpec((CHUNK, e), lambda i: (i, 0)),
        out_shape=jax.ShapeDtypeStruct((t, e), BF16),
        scratch_shapes=[pltpu.VMEM((CHUNK, e), F32)],
        compiler_params=_params("parallel"))(proj, ln_g, ln_b, w_tril, bias_t)


def _gate_a_bwd(name, proj, dy, ln_g, ln_b, w_tril, w_tril_t, bias_t):
    t, e3 = proj.shape
    e = e3 // 3
    hd = e // HEADS

    def body(p_ref, dy_ref, g_ref, b_ref, w_ref, wt_ref, bs_ref,
             dp_ref, dw_ref, dbs_ref, dg_ref, db_ref, vh_ref, vn_ref, dvn_ref):
        @pl.when(pl.program_id(0) == 0)
        def _():
            dw_ref[...] = jnp.zeros_like(dw_ref)
            dbs_ref[...] = jnp.zeros_like(dbs_ref)
            dg_ref[...] = jnp.zeros_like(dg_ref)
            db_ref[...] = jnp.zeros_like(db_ref)

        s1 = jnp.zeros((CHUNK, 1), F32)
        s2 = jnp.zeros((CHUNK, 1), F32)
        for h in range(HEADS):
            cols = slice(h * hd, (h + 1) * hd)
            vg = _gelu(p_ref[:, e + h * hd:e + (h + 1) * hd].astype(F32))
            vh_ref[:, cols] = vg
            s1 += jnp.sum(vg, axis=1, keepdims=True)
            s2 += jnp.sum(vg * vg, axis=1, keepdims=True)
        mu = s1 * (1.0 / e)
        rstd = lax.rsqrt(s2 * (1.0 / e) - mu * mu + NORM_EPS)
        m1 = jnp.zeros((CHUNK, 1), F32)
        m2 = jnp.zeros((CHUNK, 1), F32)
        for h in range(HEADS):
            cols = slice(h * hd, (h + 1) * hd)
            vhat = (vh_ref[:, cols] - mu) * rstd
            vh_ref[:, cols] = vhat
            g = g_ref[:, cols]
            vn = (vhat * g + b_ref[:, cols]).astype(BF16)
            sv = jnp.dot(w_ref[h], vn, preferred_element_type=F32) + bs_ref[h]
            u_pre = p_ref[:, cols].astype(F32)
            u = _gelu(u_pre)
            z = p_ref[:, 2 * e + h * hd:2 * e + (h + 1) * hd].astype(F32)
            sig = jax.nn.sigmoid(z)
            dyv = dy_ref[:, cols].astype(F32)
            dsgu = dyv * (z * sig)
            dp_ref[:, 2 * e + h * hd:2 * e + (h + 1) * hd] = (dyv * (u * sv) * (sig * (1.0 + z * (1.0 - sig)))).astype(BF16)
            dp_ref[:, cols] = (dsgu * sv * _gelu_grad(u_pre)).astype(BF16)
            dsv = dsgu * u
            dsv_b = dsv.astype(BF16)
            dbs_ref[h] += jnp.sum(dsv, axis=1, keepdims=True)
            dw_ref[h] += lax.dot_general(dsv_b, vn, _DOT_DIMS["nt"], preferred_element_type=F32)
            dvn = jnp.dot(wt_ref[h], dsv_b, preferred_element_type=F32)
            dvn_ref[:, cols] = dvn
            dg_ref[:, cols] += jnp.sum(dvn * vhat, axis=0, keepdims=True)
            db_ref[:, cols] += jnp.sum(dvn, axis=0, keepdims=True)
            dvhat = dvn * g
            m1 += jnp.sum(dvhat, axis=1, keepdims=True)
            m2 += jnp.sum(dvhat * vhat, axis=1, keepdims=True)
        m1 = m1 * (1.0 / e)
        m2 = m2 * (1.0 / e)
        for h in range(HEADS):
            cols = slice(h * hd, (h + 1) * hd)
            dvg = rstd * (dvn_ref[:, cols] * g_ref[:, cols] - m1 - vh_ref[:, cols] * m2)
            v_pre = p_ref[:, e + h * hd:e + (h + 1) * hd].astype(F32)
            dp_ref[:, e + h * hd:e + (h + 1) * hd] = (dvg * _gelu_grad(v_pre)).astype(BF16)

    vec = pl.BlockSpec((1, e), lambda i: (0, 0))
    mats = pl.BlockSpec((HEADS, CHUNK, CHUNK), lambda i: (0, 0, 0))
    cols1 = pl.BlockSpec((HEADS, CHUNK, 1), lambda i: (0, 0, 0))
    return pl.pallas_call(
        body, name=name, grid=(t // CHUNK,),
        in_specs=[pl.BlockSpec((CHUNK, e3), lambda i: (i, 0)), pl.BlockSpec((CHUNK, e), lambda i: (i, 0)),
                  vec, vec, mats, mats, cols1],
        out_specs=[pl.BlockSpec((CHUNK, e3), lambda i: (i, 0)), mats, cols1, vec, vec],
        out_shape=[jax.ShapeDtypeStruct((t, e3), BF16), jax.ShapeDtypeStruct((HEADS, CHUNK, CHUNK), F32),
                   jax.ShapeDtypeStruct((HEADS, CHUNK, 1), F32), jax.ShapeDtypeStruct((1, e), F32),
                   jax.ShapeDtypeStruct((1, e), F32)],
        scratch_shapes=[pltpu.VMEM((CHUNK, e), F32), pltpu.VMEM((CHUNK, e), BF16), pltpu.VMEM((CHUNK, e), F32)],
        compiler_params=_params("arbitrary"))(proj, dy, ln_g, ln_b, w_tril, w_tril_t, bias_t)


def _window_sum(xx, steps, backward):
    n = xx.shape[0]
    span = 1
    for _ in range(steps):
        xx = xx + pltpu.roll(xx, (n - span) if backward else span, axis=0)
        span *= 2
    return xx


def _pool_fwd(name, proj, e, seq, ts, tc):
    t = proj.shape[0]
    gw = e // len(WINDOWS)
    per_group = gw // tc
    hb = ts // HALO

    def body(p_ref, halo_ref, o_ref):
        i = pl.program_id(0)
        grp = pl.program_id(1) // per_group
        pos0 = (i * ts) % seq
        x = p_ref[...].astype(F32)
        halo = halo_ref[...].astype(F32) * jnp.where(pos0 == 0, 0.0, 1.0)
        xx = jnp.concatenate([halo, x], axis=0)
        pos = (pos0 + lax.broadcasted_iota(jnp.int32, (ts, 1), 0)).astype(F32)
        for k, w in enumerate(WINDOWS):
            @pl.when(grp == k)
            def _(k=k, w=w):
                win = _window_sum(xx, k + 1, False)[HALO:, :]
                o_ref[...] = (win / jnp.minimum(pos + 1.0, float(w)) - x).astype(BF16)

    return pl.pallas_call(
        body, name=name, grid=(t // ts, e // tc),
        in_specs=[pl.BlockSpec((ts, tc), lambda i, j: (i, j)),
                  pl.BlockSpec((HALO, tc), lambda i, j: (jnp.maximum(i * hb - 1, 0), j))],
        out_specs=pl.BlockSpec((ts, tc), lambda i, j: (i, j)),
        out_shape=jax.ShapeDtypeStruct((t, e), BF16),
        compiler_params=_params("parallel", "parallel"))(proj, proj)


def _pool_bwd(name, dpooled, dproj, seq, ts, tc):
    t, e = dpooled.shape
    gw = e // len(WINDOWS)
    per_group = gw // tc
    hb = ts // HALO
    last = t // HALO - 1

    def body(d_ref, halo_ref, alias_ref, o_ref):
        del alias_ref
        i = pl.program_id(0)
        grp = pl.program_id(1) // per_group
        pos0 = (i * ts) % seq
        d = d_ref[...]
        pos = (pos0 + lax.broadcasted_iota(jnp.int32, (ts, 1), 0)).astype(F32)
        at_end = ((i + 1) * ts) % seq == 0
        for k, w in enumerate(WINDOWS):
            @pl.when(grp == k)
            def _(k=k, w=w):
                r = d / jnp.minimum(pos + 1.0, float(w))
                rh = halo_ref[...] * jnp.where(at_end, 0.0, 1.0 / w)
                win = _window_sum(jnp.concatenate([r, rh], axis=0), k + 1, True)[:ts, :]
                o_ref[...] = (win - d).astype(BF16)

    return pl.pallas_call(
        body, name=name, grid=(t // ts, e // tc),
        in_specs=[pl.BlockSpec((ts, tc), lambda i, j: (i, j)),
                  pl.BlockSpec((HALO, tc), lambda i, j: (jnp.minimum((i + 1) * hb, last), j)),
                  ANY],
        out_specs=pl.BlockSpec((ts, tc), lambda i, j: (i, j)),
        out_shape=jax.ShapeDtypeStruct(dproj.shape, BF16),
        input_output_aliases={2: 0},
        compiler_params=_params("parallel", "parallel"))(dpooled, dpooled, dproj)


def _gate_b_fwd(name, q, proj, b_grp, scale, tr, tc):
    t, e = q.shape
    zoff = e // tc

    def body(q_ref, z_ref, b_ref, s_ref, y_ref):
        z = z_ref[...].astype(F32)
        mixed = (q_ref[...].astype(F32) + b_ref[...]) * s_ref[...]
        y_ref[...] = (mixed * (z * jax.nn.sigmoid(z))).astype(BF16)

    blk = pl.BlockSpec((tr, tc), lambda i, j: (i, j))
    vec = pl.BlockSpec((1, tc), lambda i, j: (0, j))
    return pl.pallas_call(
        body, name=name, grid=(t // tr, e // tc),
        in_specs=[blk, pl.BlockSpec((tr, tc), lambda i, j: (i, zoff + j)), vec, vec],
        out_specs=blk, out_shape=jax.ShapeDtypeStruct((t, e), BF16),
        compiler_params=_params("parallel", "parallel"))(q, proj, b_grp, scale)


def _gate_b_bwd(name, dy, q, proj, b_grp, scale, tr, tc):
    t, e = q.shape
    zoff = e // tc

    def body(dy_ref, q_ref, z_ref, b_ref, s_ref, dq_ref, dz_ref, ds_ref, db_ref):
        z = z_ref[...].astype(F32)
        sig = jax.nn.sigmoid(z)
        dyv = dy_ref[...].astype(F32)
        qb = q_ref[...].astype(F32) + b_ref[...]
        sc = s_ref[...]
        dmixed = dyv * (z * sig)
        dq = dmixed * sc
        dq_ref[...] = dq.astype(BF16)
        dz_ref[...] = (dyv * (qb * sc) * (sig * (1.0 + z * (1.0 - sig)))).astype(BF16)

        @pl.when(pl.program_id(1) == 0)
        def _():
            ds_ref[...] = jnp.zeros_like(ds_ref)
            db_ref[...] = jnp.zeros_like(db_ref)

        ds_ref[...] += jnp.sum(dmixed * qb, axis=0, keepdims=True)
        db_ref[...] += jnp.sum(dq, axis=0, keepdims=True)

    blk = pl.BlockSpec((tr, tc), lambda j, i: (i, j))
    zblk = pl.BlockSpec((tr, tc), lambda j, i: (i, zoff + j))
    vec = pl.BlockSpec((1, tc), lambda j, i: (0, j))
    return pl.pallas_call(
        body, name=name, grid=(e // tc, t // tr),
        in_specs=[blk, blk, zblk, vec, vec], out_specs=[blk, zblk, vec, vec],
        out_shape=[jax.ShapeDtypeStruct((t, e), BF16), jax.ShapeDtypeStruct((t, 2 * e), BF16),
                   jax.ShapeDtypeStruct((1, e), F32), jax.ShapeDtypeStruct((1, e), F32)],
        compiler_params=_params("parallel", "arbitrary"))(dy, q, proj, b_grp, scale)


def _place():
    return lax.axis_index("x"), lax.axis_index("y"), lax.axis_index("c")


def _all_gather(name, blocks):
    n = len(blocks)

    def body(*refs):
        ins, outs = refs[:n], refs[n:2 * n]
        send_sems, recv_sems, local_sems = refs[2 * n:]
        x, y, c = _place()
        me, sibling = (x, y, c), (x, y, 1 - c)
        chips = [(1 - x, y), (x, 1 - y), (1 - x, 1 - y)]

        def copy(a, k, block, to, from_input=False):
            slot = outs[a].at[4 * block[0] + 2 * block[1] + block[2]]
            return pltpu.make_async_remote_copy(
                src_ref=ins[a] if from_input else slot, dst_ref=slot,
                send_sem=send_sems.at[7 * a + k], recv_sem=recv_sems.at[7 * a + k],
                device_id=to, device_id_type=MESH_ID)

        mine = [pltpu.make_async_copy(ins[a], outs[a].at[4 * x + 2 * y + c], local_sems.at[a]) for a in range(n)]
        first = []
        for a in range(n):
            mine[a].start()
            first.append(copy(a, 0, me, sibling, True))
            first += [copy(a, 1 + j, me, (*chip, c), True) for j, chip in enumerate(chips)]
        for cp in first:
            cp.start()
        passed = []
        for j, chip in enumerate(chips):
            for a in range(n):
                copy(a, 1 + j, (*chip, c), me).wait_recv()
                cp = copy(a, 4 + j, (*chip, c), sibling)
                cp.start()
                passed.append(cp)
        for a in range(n):
            copy(a, 0, sibling, me).wait_recv()
            for j, chip in enumerate(chips):
                copy(a, 4 + j, (*chip, 1 - c), me).wait_recv()
        for cp in first + passed:
            cp.wait_send()
        for cp in mine:
            cp.wait()

    return pl.pallas_call(
        body, name=name, in_specs=[ANY] * n, out_specs=[ANY] * n,
        out_shape=[jax.ShapeDtypeStruct((NDEV,) + b.shape, b.dtype) for b in blocks],
        scratch_shapes=[pltpu.SemaphoreType.DMA((7 * n,)), pltpu.SemaphoreType.DMA((7 * n,)),
                        pltpu.SemaphoreType.DMA((n,))],
    )(*blocks)


def _exchange_sibling(name, grads):
    n = len(grads)

    def body(*refs):
        ins, outs = refs[:n], refs[n:2 * n]
        send_sems, recv_sems = refs[2 * n:]
        x, y, c = _place()
        copies = [
            pltpu.make_async_remote_copy(
                src_ref=ins[a].at[2 * q + 1 - c], dst_ref=outs[a].at[q],
                send_sem=send_sems.at[4 * a + q], recv_sem=recv_sems.at[4 * a + q],
                device_id=(x, y, 1 - c), device_id_type=MESH_ID)
            for a in range(n) for q in range(4)]
        for cp in copies:
            cp.start()
        for cp in copies:
            cp.wait()

    return pl.pallas_call(
        body, name=name, in_specs=[ANY] * n, out_specs=[ANY] * n,
        out_shape=[jax.ShapeDtypeStruct((4,) + g.shape[1:], g.dtype) for g in grads],
        scratch_shapes=[pltpu.SemaphoreType.DMA((4 * n,)), pltpu.SemaphoreType.DMA((4 * n,))],
    )(*grads)


def _exchange_chips(name, sums):
    n = len(sums)

    def body(*refs):
        ins, outs = refs[:n], refs[n:2 * n]
        send_sems, recv_sems = refs[2 * n:]
        x, y, c = _place()
        chips = [(1 - x, y), (x, 1 - y), (1 - x, 1 - y)]
        copies = [
            pltpu.make_async_remote_copy(
                src_ref=ins[a].at[j], dst_ref=outs[a].at[j],
                send_sem=send_sems.at[3 * a + j], recv_sem=recv_sems.at[3 * a + j],
                device_id=(*chips[j], c), device_id_type=MESH_ID)
            for a in range(n) for j in range(3)]
        for cp in copies:
            cp.start()
        for cp in copies:
            cp.wait()

    return pl.pallas_call(
        body, name=name, in_specs=[ANY] * n, out_specs=[ANY] * n,
        out_shape=[jax.ShapeDtypeStruct(s.shape, s.dtype) for s in sums],
        scratch_shapes=[pltpu.SemaphoreType.DMA((3 * n,)), pltpu.SemaphoreType.DMA((3 * n,))],
    )(*sums)


def _adamw(w, g, m, v):
    m = ADAM_B1 * m + (1.0 - ADAM_B1) * g
    v = ADAM_B2 * v + (1.0 - ADAM_B2) * (g * g)
    m_hat = m / (1.0 - ADAM_B1 ** ADAM_STEP)
    v_hat = v / (1.0 - ADAM_B2 ** ADAM_STEP)
    delta = -ADAM_LR * (m_hat / (jnp.sqrt(v_hat) + ADAM_EPS) + ADAM_WD * w)
    return delta, m, v


def _row_tile(rows, cols):
    return _tile(rows, max(F32_SUBLANES, (3 * 128 * 1024) // cols), F32_SUBLANES)


def _chip_sums(name, ids, grad, from_sibling):
    _, r, c = grad.shape
    tr = _row_tile(r, c)

    def body(ids_ref, g_ref, s_ref, o_ref):
        del ids_ref
        o_ref[...] = (g_ref[...].astype(F32) + s_ref[...].astype(F32)).astype(o_ref.dtype)

    return pl.pallas_call(
        body, name=name,
        grid_spec=pltpu.PrefetchScalarGridSpec(
            num_scalar_prefetch=1, grid=(3, r // tr),
            in_specs=[pl.BlockSpec((None, tr, c), lambda j, i, ids: (2 * ids[j] + ids[3], i, 0)),
                      pl.BlockSpec((None, tr, c), lambda j, i, ids: (ids[j], i, 0))],
            out_specs=pl.BlockSpec((None, tr, c), lambda j, i, ids: (j, i, 0))),
        out_shape=jax.ShapeDtypeStruct((3, r, c), grad.dtype),
        compiler_params=_params("parallel", "parallel"))(ids, grad, from_sibling)


def _reduce_adamw(name, ids, grad, from_sibling, from_chips, w, m, v):
    _, r, c = grad.shape
    tr = _row_tile(r, c)

    def body(ids_ref, g_ref, s_ref, c_ref, w_ref, m_ref, v_ref, og_ref, od_ref, om_ref, ov_ref):
        del ids_ref
        g = g_ref[...].astype(F32) + s_ref[...].astype(F32)
        for j in range(3):
            g = g + c_ref[j].astype(F32)
        og_ref[...] = g
        od_ref[...], om_ref[...], ov_ref[...] = _adamw(w_ref[...], g, m_ref[...], v_ref[...])

    blk = pl.BlockSpec((tr, c), lambda i, ids: (i, 0))
    out = jax.ShapeDtypeStruct((r, c), F32)
    return pl.pallas_call(
        body, name=name,
        grid_spec=pltpu.PrefetchScalarGridSpec(
            num_scalar_prefetch=1, grid=(r // tr,),
            in_specs=[pl.BlockSpec((None, tr, c), lambda i, ids: (ids[5], i, 0)),
                      pl.BlockSpec((None, tr, c), lambda i, ids: (ids[4], i, 0)),
                      pl.BlockSpec((3, tr, c), lambda i, ids: (0, i, 0)), blk, blk, blk],
            out_specs=[blk, blk, blk, blk]),
        out_shape=[out, out, out, out],
        compiler_params=_params("parallel"))(ids, grad, from_sibling, from_chips, w, m, v)


def _small_adamw(name, gathered, mask, w, m, v):
    _, r, c = gathered.shape

    def body(g_ref, k_ref, w_ref, m_ref, v_ref, og_ref, od_ref, om_ref, ov_ref):
        g = g_ref[0]
        for dev in range(1, NDEV):
            g = g + g_ref[dev]
        g = g * k_ref[...]
        og_ref[...] = g
        od_ref[...], om_ref[...], ov_ref[...] = _adamw(w_ref[...], g, m_ref[...], v_ref[...])

    out = jax.ShapeDtypeStruct((r, c), F32)
    return pl.pallas_call(
        body, name=name, out_shape=[out, out, out, out],
        compiler_params=pltpu.CompilerParams(vmem_limit_bytes=VMEM_LIMIT))(gathered, mask, w, m, v)


def kernel(x, pre_norm, post_norm, a_w_in, a_ln_g, a_ln_b, a_w_s, a_b_s, a_w_out, b_w_in, b_w_grp, b_b_grp, b_scale, b_w_out, loss_target, m_pre_norm, m_post_norm, m_a_w_in, m_a_ln_g, m_a_ln_b, m_a_w_s, m_a_b_s, m_a_w_out, m_b_w_in, m_b_w_grp, m_b_b_grp, m_b_scale, m_b_w_out, v_pre_norm, v_post_norm, v_a_w_in, v_a_ln_g, v_a_ln_b, v_a_w_s, v_a_b_s, v_a_w_out, v_b_w_in, v_b_w_grp, v_b_b_grp, v_b_scale, v_b_w_out):
    bl, seq, d = x.shape
    t = bl * seq
    e = a_w_out.shape[1] * NDEV
    groups = len(WINDOWS)
    gw = e // groups
    c3, c2 = a_w_in.shape[2], b_w_in.shape[2]
    assert e // HEADS % LANES == 0 and seq % CHUNK == 0 and (2 * d // LANES) % F32_SUBLANES == 0

    bm = _tile(t, 1024)
    bn = 1024
    bd = _tile(d, 1024)
    be = _tile(e, 1024)
    bgw = _tile(gw, 1024)
    bk = _tile(e, 4096)
    tr = _tile(t, 128, F32_SUBLANES)
    ts = _tile(seq, 512, HALO)
    tc = _tile(gw, 1024)
    tre = _tile(t, 256, HALO)

    xc, yc, cc = _place()
    my_chip = 2 * xc + yc
    ids = jnp.stack([2 * (1 - xc) + yc, 2 * xc + (1 - yc), 2 * (1 - xc) + (1 - yc), cc, my_chip,
                     2 * my_chip + cc, 0, 0]).astype(jnp.int32)

    (wa_in, wa_out, wb_in, wb_grp, wb_out, bgrp_g, scale_g) = _all_gather("gather_weights", [
        a_w_in[0].astype(BF16), a_w_out[0].astype(BF16), b_w_in[0].astype(BF16), b_w_grp[0].astype(BF16),
        b_w_out[0].astype(BF16), b_b_grp[0], b_scale])
    wa_out = wa_out.reshape(e, d)
    wb_out = wb_out.reshape(e, d)
    wb_grp = jnp.transpose(wb_grp, (1, 0, 2, 3)).reshape(groups, gw, gw)
    bgrp = jnp.transpose(bgrp_g, (1, 0, 2)).reshape(1, e)
    scale = scale_g.reshape(1, e)
    w_tril = jnp.tril(a_w_s[0]).astype(BF16)
    w_tril_t = jnp.swapaxes(w_tril, 1, 2)
    bias_t = a_b_s[0][:, :, None]

    x2d = x.reshape(t, d)
    tgt = loss_target.reshape(t, d)

    h0 = _rms_fwd("pre_norm_a", x2d, pre_norm[0:1], tr)
    proj_a = _mm_in_proj("a_in_proj", h0, wa_in, bm, bn)
    y_a = _gate_a_fwd("a_gate", proj_a, a_ln_g, a_ln_b, w_tril, bias_t)
    m_a = _mm_out_proj("a_out_proj", y_a, wa_out, bm, bd, bk)
    x1, h1 = _post_pre("post_a_pre_b", x2d, m_a, post_norm[0:1], pre_norm[1:2], tr)
    proj_b = _mm_in_proj("b_in_proj", h1, wb_in, bm, bn)
    pooled = _pool_fwd("b_pool", proj_b, e, seq, ts, tc)
    q = _mm_grouped("b_group_proj", pooled, wb_grp, "nn", bm, bgw, BF16)
    y_b = _gate_b_fwd("b_gate", q, proj_b, bgrp, scale, tre, tc)
    m_b = _mm_out_proj("b_out_proj", y_b, wb_out, bm, bd, bk)
    sq, dres, dm_b, dg_post1 = _loss_head("loss_head", x1, m_b, tgt, post_norm[1:2], tr)
    loss = lax.psum(0.5 * jnp.sum(sq) / d, ("x", "y", "c"))

    dy_b = _mm_dact_out("b_out_dact", dm_b, wb_out, bm, be)
    dwb_out = _mm_dw("b_out_dw", y_b, dm_b, be, bd)
    dq, dproj_b, dscale, dbgrp = _gate_b_bwd("b_gate_bwd", dy_b, q, proj_b, bgrp, scale, tre, tc)
    dwb_grp = _mm_dw_grouped("b_group_dw", pooled, dq, groups, bgw, bgw)
    dpooled = _mm_grouped("b_group_dact", dq, wb_grp, "nt", bm, bgw, F32)
    dproj_b = _pool_bwd("b_pool_bwd", dpooled, dproj_b, seq, ts, tc)
    dh1 = _mm_dact_in("b_in_dact", dproj_b, wb_in, bm, bd)
    dwb_in = _mm_dw_sharded("b_in_dw", h1, dproj_b, c2, bd, bn)
    dres, dm_a, dg_pre1, dg_post0 = _norms_bwd_mid("norms_bwd_mid", dres, dh1, x1, pre_norm[1:2], m_a, post_norm[0:1], tr)

    dy_a = _mm_dact_out("a_out_dact", dm_a, wa_out, bm, be)
    dwa_out = _mm_dw("a_out_dw", y_a, dm_a, be, bd)
    dproj_a, dws, dbs, dlng, dlnb = _gate_a_bwd("a_gate_bwd", proj_a, dy_a, a_ln_g, a_ln_b, w_tril, w_tril_t, bias_t)
    dh0 = _mm_dact_in("a_in_dact", dproj_a, wa_in, bm, bd)
    dwa_in = _mm_dw_sharded("a_in_dw", h0, dproj_a, c3, bd, bn)
    grad_x, dg_pre0 = _norm_bwd_first("norm_bwd_first", dres, dh0, x2d, pre_norm[0:1], tr)

    sharded = [
        ("a_w_in", dwa_in, a_w_in, m_a_w_in, v_a_w_in),
        ("a_w_out", dwa_out.reshape(NDEV, e // NDEV, d), a_w_out, m_a_w_out, v_a_w_out),
        ("b_w_in", dwb_in, b_w_in, m_b_w_in, v_b_w_in),
        ("b_w_grp", jnp.transpose(dwb_grp.reshape(groups, NDEV, gw // NDEV, gw), (1, 0, 2, 3)).reshape(NDEV, groups * gw // NDEV, gw),
         b_w_grp, m_b_w_grp, v_b_w_grp),
        ("b_w_out", dwb_out.reshape(NDEV, e // NDEV, d), b_w_out, m_b_w_out, v_b_w_out),
        ("b_b_grp", jnp.transpose(dbgrp.reshape(groups, NDEV, gw // NDEV), (1, 0, 2)), b_b_grp, m_b_b_grp, v_b_b_grp),
        ("b_scale", dscale.reshape(NDEV, 1, e // NDEV), b_scale, m_b_scale, v_b_scale),
    ]
    grads = [s[1] for s in sharded]
    from_sibling = _exchange_sibling("grads_to_sibling", grads)
    sums = [_chip_sums("chip_sum_" + s[0], ids, g, fs) for s, g, fs in zip(sharded, grads, from_sibling)]
    from_chips = _exchange_chips("grads_to_chips", sums)
    results = {}
    for (wname, g, w, m, v), fs, fc in zip(sharded, from_sibling, from_chips):
        shape2d = g.shape[1:]
        outs = _reduce_adamw("adamw_" + wname, ids, g, fs, fc, w.reshape(shape2d), m.reshape(shape2d), v.reshape(shape2d))
        results[wname] = [o.reshape(w.shape) for o in outs]

    small = [
        ("pre_norm", jnp.concatenate([dg_pre0, dg_pre1], axis=0), pre_norm, m_pre_norm, v_pre_norm),
        ("post_norm", jnp.concatenate([dg_post0, dg_post1], axis=0), post_norm, m_post_norm, v_post_norm),
        ("a_ln_g", dlng, a_ln_g, m_a_ln_g, v_a_ln_g),
        ("a_ln_b", dlnb, a_ln_b, m_a_ln_b, v_a_ln_b),
        ("a_w_s", dws, a_w_s, m_a_w_s, v_a_w_s),
        ("a_b_s", dbs, a_b_s, m_a_b_s, v_a_b_s),
    ]

    def pack(arrays):
        return jnp.concatenate([a.reshape(-1, LANES) for a in arrays], axis=0)

    tril_mask = jnp.broadcast_to(jnp.tril(jnp.ones((CHUNK, CHUNK), F32)), a_w_s.shape)
    mask = pack([tril_mask if s[0] == "a_w_s" else jnp.ones(s[2].shape, F32) for s in small])
    (gathered,) = _all_gather("gather_small_grads", [pack([s[1] for s in small])])
    packed = _small_adamw("adamw_small", gathered, mask, pack([s[2] for s in small]),
                          pack([s[3] for s in small]), pack([s[4] for s in small]))
    row = 0
    for wname, _, w, _, _ in small:
        rows = w.size // LANES
        results[wname] = [p[row:row + rows].reshape(w.shape) for p in packed]
        row += rows

    order = ["pre_norm", "post_norm", "a_w_in", "a_ln_g", "a_ln_b", "a_w_s", "a_b_s", "a_w_out",
             "b_w_in", "b_w_grp", "b_b_grp", "b_scale", "b_w_out"]
    out = [loss, grad_x.reshape(x.shape)]
    for kind in range(4):
        out += [results[wname][kind] for wname in order]
    return tuple(out)
```

```python
import functools
import math

import jax
import jax.numpy as jnp
from jax import lax
from jax.experimental import pallas as pl
from jax.experimental.pallas import tpu as pltpu

NDEV = 8
CHUNK = 128
HEADS = 16
WINDOWS = (2, 4, 8, 16)
HALO = 16
NORM_EPS = 1e-6
ADAM_LR, ADAM_B1, ADAM_B2, ADAM_EPS, ADAM_WD, ADAM_STEP = 0.001, 0.9, 0.999, 1e-08, 0.01, 10
LANES = 128
F32_SUBLANES = 8
VMEM_LIMIT = 60 * 1024 * 1024
BF16 = jnp.bfloat16
F32 = jnp.float32
MESH_ID = pl.DeviceIdType.MESH
ANY = pl.BlockSpec(memory_space=pl.ANY)


def _tile(dim, pref, mult=LANES):
    t = (min(pref, dim) // mult) * mult
    while t >= mult:
        if dim % t == 0:
            return t
        t -= mult
    return dim


def _params(*sem):
    return pltpu.CompilerParams(dimension_semantics=sem, vmem_limit_bytes=VMEM_LIMIT)


def _gelu(x):
    return 0.5 * x * (1.0 + lax.erf(x * (1.0 / math.sqrt(2.0))))


def _gelu_grad(x):
    return 0.5 * (1.0 + lax.erf(x * (1.0 / math.sqrt(2.0)))) + x * jnp.exp(-0.5 * x * x) * (1.0 / math.sqrt(2.0 * math.pi))


class _Job:
    def __init__(self, arrays, out_shapes, aliases, n_remote, n_local, copies):
        self.arrays, self.out_shapes, self.aliases = list(arrays), list(out_shapes), dict(aliases)
        self.n_remote, self.n_local, self.copies = n_remote, n_local, copies


def _call(name, body, *, grid, in_specs, out_specs, out_shape, args, sem, scratch=(), jobs=(), prefetch=None):
    n_in, n_out, n_scr = len(in_specs), len(out_specs), len(scratch)
    job_in = [a for job in jobs for a in job.arrays]
    job_out = [s for job in jobs for s in job.out_shapes]
    aliases = {}
    i_off, o_off = n_in, n_out
    for job in jobs:
        for i, o in job.aliases.items():
            aliases[i_off + i] = o_off + o
        i_off += len(job.arrays)
        o_off += len(job.out_shapes)
    n_remote = sum(job.n_remote for job in jobs)
    n_local = sum(job.n_local for job in jobs)
    sems = [pltpu.SemaphoreType.DMA((max(n_remote, 1),)), pltpu.SemaphoreType.DMA((max(n_remote, 1),)),
            pltpu.SemaphoreType.DMA((max(n_local, 1),))] if jobs else []

    def wrapped(*refs):
        if prefetch is not None:
            refs = refs[1:]
        ins, jins = refs[:n_in], refs[n_in:n_in + len(job_in)]
        rest = refs[n_in + len(job_in):]
        outs, jouts = rest[:n_out], rest[n_out:n_out + len(job_out)]
        scr = rest[n_out + len(job_out):n_out + len(job_out) + n_scr]

        def all_copies():
            send_sems, recv_sems, local_sems = rest[-3:]
            remote, local = [], []
            i0 = o0 = r0 = l0 = 0
            for job in jobs:
                r, l = job.copies(
                    jins[i0:i0 + len(job.arrays)], jouts[o0:o0 + len(job.out_shapes)],
                    lambda k, r0=r0: (send_sems.at[r0 + k], recv_sems.at[r0 + k]),
                    lambda k, l0=l0: local_sems.at[l0 + k])
                remote += r
                local += l
                i0 += len(job.arrays)
                o0 += len(job.out_shapes)
                r0 += job.n_remote
                l0 += job.n_local
            return remote + local

        if jobs:
            first = functools.reduce(jnp.logical_and, [pl.program_id(i) == 0 for i in range(len(grid))])

            @pl.when(first)
            def _():
                for cp in all_copies():
                    cp.start()

        body(*ins, *outs, *scr)

        if jobs:
            last = functools.reduce(jnp.logical_and, [pl.program_id(i) == grid[i] - 1 for i in range(len(grid))])

            @pl.when(last)
            def _():
                for cp in all_copies():
                    cp.wait()

    all_in = list(in_specs) + [ANY] * len(job_in)
    all_out = list(out_specs) + [ANY] * len(job_out)
    shapes = list(out_shape) + job_out
    operands = list(args) + job_in
    if prefetch is None:
        res = pl.pallas_call(
            wrapped, name=name, grid=grid, in_specs=all_in, out_specs=all_out, out_shape=shapes,
            scratch_shapes=list(scratch) + sems, input_output_aliases=aliases, compiler_params=_params(*sem))(*operands)
    else:
        assert not aliases
        res = pl.pallas_call(
            wrapped, name=name,
            grid_spec=pltpu.PrefetchScalarGridSpec(
                num_scalar_prefetch=1, grid=grid, in_specs=all_in, out_specs=all_out,
                scratch_shapes=list(scratch) + sems),
            out_shape=shapes, compiler_params=_params(*sem))(prefetch, *operands)
    return list(res[:n_out]), list(res[n_out:])


_DOT_DIMS = {"nn": (((1,), (0,)), ((), ())), "nt": (((1,), (1,)), ((), ())), "tn": (((0,), (0,)), ((), ()))}


def _matmul(name, a, b, *, mode, grid, a_spec, b_spec, o_spec, out_shape, nk, jobs=(), prefetch=None):
    dims = _DOT_DIMS[mode]
    acc_shape = tuple(d for d in o_spec.block_shape if d is not None)

    if nk == 1:
        def body(a_ref, b_ref, o_ref):
            o_ref[...] = lax.dot_general(a_ref[...], b_ref[...], dims, preferred_element_type=F32).astype(o_ref.dtype)
        scratch = []
        sem = ("parallel",) * len(grid) if not jobs else ("arbitrary",) * len(grid)
    else:
        def body(a_ref, b_ref, o_ref, acc_ref):
            k = pl.program_id(len(grid) - 1)
            part = lax.dot_general(a_ref[...], b_ref[...], dims, preferred_element_type=F32)

            @pl.when(k == 0)
            def _():
                acc_ref[...] = part

            @pl.when(jnp.logical_and(k > 0, k < nk - 1))
            def _():
                acc_ref[...] += part

            @pl.when(k == nk - 1)
            def _():
                o_ref[...] = (acc_ref[...] + part).astype(o_ref.dtype)
        scratch = [pltpu.VMEM(acc_shape, F32)]
        sem = (("parallel",) * (len(grid) - 1) if not jobs else ("arbitrary",) * (len(grid) - 1)) + ("arbitrary",)

    outs, job_outs = _call(name, body, grid=grid, in_specs=[a_spec, b_spec], out_specs=[o_spec], out_shape=[out_shape],
                           args=[a, b], sem=sem, scratch=scratch, jobs=jobs, prefetch=prefetch)
    return outs[0], job_outs


def _mm_in_proj(name, h, w_g, bm, bn, jobs=()):
    t, d = h.shape
    _, _, c = w_g.shape
    bn = _tile(c, bn)
    nb = c // bn
    return _matmul(
        name, h, w_g, mode="nn", grid=(t // bm, NDEV * nb), nk=1, jobs=jobs,
        a_spec=pl.BlockSpec((bm, d), lambda i, j: (i, 0)),
        b_spec=pl.BlockSpec((None, d, bn), lambda i, j: (j // nb, 0, j % nb)),
        o_spec=pl.BlockSpec((bm, bn), lambda i, j: (i, j)),
        out_shape=jax.ShapeDtypeStruct((t, NDEV * c), BF16))


def _mm_out_proj(name, y, w, bm, bn, bk, jobs=()):
    t, e = y.shape
    _, d = w.shape
    nk = e // bk
    return _matmul(
        name, y, w, mode="nn", grid=(t // bm, d // bn, nk), nk=nk, jobs=jobs,
        a_spec=pl.BlockSpec((bm, bk), lambda i, j, k: (i, k)),
        b_spec=pl.BlockSpec((bk, bn), lambda i, j, k: (k, j)),
        o_spec=pl.BlockSpec((bm, bn), lambda i, j, k: (i, j)),
        out_shape=jax.ShapeDtypeStruct((t, d), F32))


def _mm_grouped(name, a, w, mode, bm, bn, out_dtype, jobs=()):
    t, e = a.shape
    g_n, gw, _ = w.shape
    nb = gw // bn
    if mode == "nn":
        b_spec = pl.BlockSpec((None, gw, bn), lambda g, i, j: (g, 0, j))
    else:
        b_spec = pl.BlockSpec((None, bn, gw), lambda g, i, j: (g, j, 0))
    return _matmul(
        name, a, w, mode=mode, grid=(g_n, t // bm, nb), nk=1, jobs=jobs,
        a_spec=pl.BlockSpec((bm, gw), lambda g, i, j: (i, g)),
        b_spec=b_spec,
        o_spec=pl.BlockSpec((bm, bn), lambda g, i, j: (i, g * nb + j)),
        out_shape=jax.ShapeDtypeStruct((t, e), out_dtype))


def _mm_dact_out(name, dm, w, bm, bn, jobs=()):
    t, d = dm.shape
    e, _ = w.shape
    return _matmul(
        name, dm, w, mode="nt", grid=(t // bm, e // bn), nk=1, jobs=jobs,
        a_spec=pl.BlockSpec((bm, d), lambda i, j: (i, 0)),
        b_spec=pl.BlockSpec((bn, d), lambda i, j: (j, 0)),
        o_spec=pl.BlockSpec((bm, bn), lambda i, j: (i, j)),
        out_shape=jax.ShapeDtypeStruct((t, e), BF16))


def _mm_dact_in(name, dproj, w_g, bm, bn, jobs=()):
    t, _ = dproj.shape
    _, d, c = w_g.shape
    return _matmul(
        name, dproj, w_g, mode="nt", grid=(t // bm, d // bn, NDEV), nk=NDEV, jobs=jobs,
        a_spec=pl.BlockSpec((bm, c), lambda i, j, k: (i, k)),
        b_spec=pl.BlockSpec((None, bn, c), lambda i, j, k: (k, j, 0)),
        o_spec=pl.BlockSpec((bm, bn), lambda i, j, k: (i, j)),
        out_shape=jax.ShapeDtypeStruct((t, d), F32))


def _mm_dw(name, a, b, bm, bn, jobs=()):
    t, m = a.shape
    _, n = b.shape
    return _matmul(
        name, a, b, mode="tn", grid=(m // bm, n // bn), nk=1, jobs=jobs,
        a_spec=pl.BlockSpec((t, bm), lambda i, j: (0, i)),
        b_spec=pl.BlockSpec((t, bn), lambda i, j: (0, j)),
        o_spec=pl.BlockSpec((bm, bn), lambda i, j: (i, j)),
        out_shape=jax.ShapeDtypeStruct((m, n), BF16))


def _mm_dw_sharded(name, a, b, c, bm, bn, jobs=()):
    t, m = a.shape
    bn = _tile(c, bn)
    nb = c // bn
    return _matmul(
        name, a, b, mode="tn", grid=(m // bm, NDEV * nb), nk=1, jobs=jobs,
        a_spec=pl.BlockSpec((t, bm), lambda i, j: (0, i)),
        b_spec=pl.BlockSpec((t, bn), lambda i, j: (0, j)),
        o_spec=pl.BlockSpec((None, bm, bn), lambda i, j: (j // nb, i, j % nb)),
        out_shape=jax.ShapeDtypeStruct((NDEV, m, c), BF16))


def _mm_dw_class(name, ids, cls_slot, a, b, c, bm, bn, jobs=()):
    t, m = a.shape
    bn = _tile(c, bn)
    nb = c // bn
    return _matmul(
        name, a, b, mode="tn", grid=(m // bm, 4 * nb), nk=1, jobs=jobs, prefetch=ids,
        a_spec=pl.BlockSpec((t, bm), lambda i, j, ids: (0, i)),
        b_spec=pl.BlockSpec((t, bn), lambda i, j, ids: (0, (2 * (j // nb) + ids[cls_slot]) * nb + j % nb)),
        o_spec=pl.BlockSpec((None, bm, bn), lambda i, j, ids: (j // nb, i, j % nb)),
        out_shape=jax.ShapeDtypeStruct((4, m, c), BF16))


def _mm_dw_grouped(name, a, b, g_n, bm, bn, jobs=()):
    t, e = a.shape
    gw = e // g_n
    nbm, nbn = gw // bm, gw // bn
    return _matmul(
        name, a, b, mode="tn", grid=(g_n, nbm, nbn), nk=1, jobs=jobs,
        a_spec=pl.BlockSpec((t, bm), lambda g, i, j: (0, g * nbm + i)),
        b_spec=pl.BlockSpec((t, bn), lambda g, i, j: (0, g * nbn + j)),
        o_spec=pl.BlockSpec((None, bm, bn), lambda g, i, j: (g, i, j)),
        out_shape=jax.ShapeDtypeStruct((g_n, gw, gw), BF16))


def _rms(x):
    return lax.rsqrt(jnp.mean(x * x, axis=-1, keepdims=True) + NORM_EPS)


def _rms_bwd(dy, x, g):
    r = _rms(x)
    xhat = x * r
    dxhat = dy * g
    dx = r * (dxhat - xhat * jnp.mean(dxhat * xhat, axis=-1, keepdims=True))
    return dx, dy * xhat


def _rms_fwd(name, x, g, tr):
    t, d = x.shape

    def body(x_ref, g_ref, h_ref):
        xv = x_ref[...]
        h_ref[...] = (xv * _rms(xv) * g_ref[...]).astype(BF16)

    row = pl.BlockSpec((tr, d), lambda i: (i, 0))
    vec = pl.BlockSpec((1, d), lambda i: (0, 0))
    return pl.pallas_call(
        body, name=name, grid=(t // tr,), in_specs=[row, vec], out_specs=row,
        out_shape=jax.ShapeDtypeStruct((t, d), BF16), compiler_params=_params("parallel"))(x, g)


def _post_pre(name, x, m, g_post, g_pre, tr):
    t, d = x.shape

    def body(x_ref, m_ref, gp_ref, gn_ref, x1_ref, h_ref):
        mv = m_ref[...]
        x1 = x_ref[...] + mv * _rms(mv) * gp_ref[...]
        x1_ref[...] = x1
        h_ref[...] = (x1 * _rms(x1) * gn_ref[...]).astype(BF16)

    row = pl.BlockSpec((tr, d), lambda i: (i, 0))
    vec = pl.BlockSpec((1, d), lambda i: (0, 0))
    return pl.pallas_call(
        body, name=name, grid=(t // tr,), in_specs=[row, row, vec, vec], out_specs=[row, row],
        out_shape=[jax.ShapeDtypeStruct((t, d), F32), jax.ShapeDtypeStruct((t, d), BF16)],
        compiler_params=_params("parallel"))(x, m, g_post, g_pre)


def _loss_head(name, x1, m, tgt, g_post, tr):
    t, d = x1.shape

    def body(x1_ref, m_ref, t_ref, g_ref, sq_ref, dres_ref, dm_ref, dg_ref):
        mv = m_ref[...]
        g = g_ref[...]
        diff = x1_ref[...] + mv * _rms(mv) * g - t_ref[...]
        dout = diff * (1.0 / d)
        dm, dg_rows = _rms_bwd(dout, mv, g)
        dres_ref[...] = dout
        dm_ref[...] = dm.astype(BF16)

        @pl.when(pl.program_id(0) == 0)
        def _():
            sq_ref[...] = jnp.zeros_like(sq_ref)
            dg_ref[...] = jnp.zeros_like(dg_ref)

        sq_ref[...] += jnp.sum(diff * diff, axis=0, keepdims=True)
        dg_ref[...] += jnp.sum(dg_rows, axis=0, keepdims=True)

    row = pl.BlockSpec((tr, d), lambda i: (i, 0))
    vec = pl.BlockSpec((1, d), lambda i: (0, 0))
    return pl.pallas_call(
        body, name=name, grid=(t // tr,), in_specs=[row, row, row, vec], out_specs=[vec, row, row, vec],
        out_shape=[jax.ShapeDtypeStruct((1, d), F32), jax.ShapeDtypeStruct((t, d), F32),
                   jax.ShapeDtypeStruct((t, d), BF16), jax.ShapeDtypeStruct((1, d), F32)],
        compiler_params=_params("arbitrary"))(x1, m, tgt, g_post)


def _norms_bwd_mid(name, dres, dh, x1, g_pre, m, g_post, tr):
    t, d = x1.shape

    def body(dres_ref, dh_ref, x1_ref, gpre_ref, m_ref, gpost_ref, out_ref, dm_ref, dgpre_ref, dgpost_ref):
        dx, dgpre_rows = _rms_bwd(dh_ref[...], x1_ref[...], gpre_ref[...])
        dres2 = dres_ref[...] + dx
        out_ref[...] = dres2
        dm, dgpost_rows = _rms_bwd(dres2, m_ref[...], gpost_ref[...])
        dm_ref[...] = dm.astype(BF16)

        @pl.when(pl.program_id(0) == 0)
        def _():
            dgpre_ref[...] = jnp.zeros_like(dgpre_ref)
            dgpost_ref[...] = jnp.zeros_like(dgpost_ref)

        dgpre_ref[...] += jnp.sum(dgpre_rows, axis=0, keepdims=True)
        dgpost_ref[...] += jnp.sum(dgpost_rows, axis=0, keepdims=True)

    row = pl.BlockSpec((tr, d), lambda i: (i, 0))
    vec = pl.BlockSpec((1, d), lambda i: (0, 0))
    return pl.pallas_call(
        body, name=name, grid=(t // tr,), in_specs=[row, row, row, vec, row, vec], out_specs=[row, row, vec, vec],
        out_shape=[jax.ShapeDtypeStruct((t, d), F32), jax.ShapeDtypeStruct((t, d), BF16),
                   jax.ShapeDtypeStruct((1, d), F32), jax.ShapeDtypeStruct((1, d), F32)],
        compiler_params=_params("arbitrary"))(dres, dh, x1, g_pre, m, g_post)


def _norm_bwd_first(name, dres, dh, x, g_pre, tr):
    t, d = x.shape

    def body(dres_ref, dh_ref, x_ref, g_ref, out_ref, dg_ref):
        dx, dg_rows = _rms_bwd(dh_ref[...], x_ref[...], g_ref[...])
        out_ref[...] = dres_ref[...] + dx

        @pl.when(pl.program_id(0) == 0)
        def _():
            dg_ref[...] = jnp.zeros_like(dg_ref)

        dg_ref[...] += jnp.sum(dg_rows, axis=0, keepdims=True)

    row = pl.BlockSpec((tr, d), lambda i: (i, 0))
    vec = pl.BlockSpec((1, d), lambda i: (0, 0))
    return pl.pallas_call(
        body, name=name, grid=(t // tr,), in_specs=[row, row, row, vec], out_specs=[row, vec],
        out_shape=[jax.ShapeDtypeStruct((t, d), F32), jax.ShapeDtypeStruct((1, d), F32)],
        compiler_params=_params("arbitrary"))(dres, dh, x, g_pre)


def _gate_a_fwd(name, proj, ln_g, ln_b, w_tril, bias_t, jobs=()):
    t, e3 = proj.shape
    e = e3 // 3
    hd = e // HEADS

    def body(p_ref, g_ref, b_ref, w_ref, bs_ref, y_ref, vg_ref):
        s1 = jnp.zeros((CHUNK, 1), F32)
        s2 = jnp.zeros((CHUNK, 1), F32)
        for h in range(HEADS):
            cols = slice(h * hd, (h + 1) * hd)
            vg = _gelu(p_ref[:, e + h * hd:e + (h + 1) * hd].astype(F32))
            vg_ref[:, cols] = vg
            s1 += jnp.sum(vg, axis=1, keepdims=True)
            s2 += jnp.sum(vg * vg, axis=1, keepdims=True)
        mu = s1 * (1.0 / e)
        rstd = lax.rsqrt(s2 * (1.0 / e) - mu * mu + NORM_EPS)
        for h in range(HEADS):
            cols = slice(h * hd, (h + 1) * hd)
            vn = ((vg_ref[:, cols] - mu) * rstd * g_ref[:, cols] + b_ref[:, cols]).astype(BF16)
            sv = jnp.dot(w_ref[h], vn, preferred_element_type=F32) + bs_ref[h]
            u = _gelu(p_ref[:, cols].astype(F32))
            z = p_ref[:, 2 * e + h * hd:2 * e + (h + 1) * hd].astype(F32)
            y_ref[:, cols] = (u * sv * (z * jax.nn.sigmoid(z))).astype(BF16)

    vec = pl.BlockSpec((1, e), lambda i: (0, 0))
    outs, job_outs = _call(
        name, body, grid=(t // CHUNK,),
        in_specs=[pl.BlockSpec((CHUNK, e3), lambda i: (i, 0)), vec, vec,
                  pl.BlockSpec((HEADS, CHUNK, CHUNK), lambda i: (0, 0, 0)),
                  pl.BlockSpec((HEADS, CHUNK, 1), lambda i: (0, 0, 0))],
        out_specs=[pl.BlockSpec((CHUNK, e), lambda i: (i, 0))],
        out_shape=[jax.ShapeDtypeStruct((t, e), BF16)],
        scratch=[pltpu.VMEM((CHUNK, e), F32)],
        sem=("arbitrary",) if jobs else ("parallel",),
        args=[proj, ln_g, ln_b, w_tril, bias_t], jobs=jobs)
    return outs[0], job_outs


def _gate_a_bwd(name, proj, dy, ln_g, ln_b, w_tril, w_tril_t, bias_t, jobs=()):
    t, e3 = proj.shape
    e = e3 // 3
    hd = e // HEADS

    def body(p_ref, dy_ref, g_ref, b_ref, w_ref, wt_ref, bs_ref,
             dp_ref, dw_ref, dbs_ref, dg_ref, db_ref, vh_ref, dvn_ref):
        @pl.when(pl.program_id(0) == 0)
        def _():
            dw_ref[...] = jnp.zeros_like(dw_ref)
            dbs_ref[...] = jnp.zeros_like(dbs_ref)
            dg_ref[...] = jnp.zeros_like(dg_ref)
            db_ref[...] = jnp.zeros_like(db_ref)

        s1 = jnp.zeros((CHUNK, 1), F32)
        s2 = jnp.zeros((CHUNK, 1), F32)
        for h in range(HEADS):
            cols = slice(h * hd, (h + 1) * hd)
            vg = _gelu(p_ref[:, e + h * hd:e + (h + 1) * hd].astype(F32))
            vh_ref[:, cols] = vg
            s1 += jnp.sum(vg, axis=1, keepdims=True)
            s2 += jnp.sum(vg * vg, axis=1, keepdims=True)
        mu = s1 * (1.0 / e)
        rstd = lax.rsqrt(s2 * (1.0 / e) - mu * mu + NORM_EPS)
        m1 = jnp.zeros((CHUNK, 1), F32)
        m2 = jnp.zeros((CHUNK, 1), F32)
        for h in range(HEADS):
            cols = slice(h * hd, (h + 1) * hd)
            vhat = (vh_ref[:, cols] - mu) * rstd
            vh_ref[:, cols] = vhat
            g = g_ref[:, cols]
            vn = (vhat * g + b_ref[:, cols]).astype(BF16)
            sv = jnp.dot(w_ref[h], vn, preferred_element_type=F32) + bs_ref[h]
            u_pre = p_ref[:, cols].astype(F32)
            u = _gelu(u_pre)
            z = p_ref[:, 2 * e + h * hd:2 * e + (h + 1) * hd].astype(F32)
            sig = jax.nn.sigmoid(z)
            dyv = dy_ref[:, cols].astype(F32)
            dsgu = dyv * (z * sig)
            dp_ref[:, 2 * e + h * hd:2 * e + (h + 1) * hd] = (dyv * (u * sv) * (sig * (1.0 + z * (1.0 - sig)))).astype(BF16)
            dp_ref[:, cols] = (dsgu * sv * _gelu_grad(u_pre)).astype(BF16)
            dsv = dsgu * u
            dsv_b = dsv.astype(BF16)
            dbs_ref[h] += jnp.sum(dsv, axis=1, keepdims=True)
            dw_ref[h] += lax.dot_general(dsv_b, vn, _DOT_DIMS["nt"], preferred_element_type=F32)
            dvn = jnp.dot(wt_ref[h], dsv_b, preferred_element_type=F32)
            dvn_ref[:, cols] = dvn
            dg_ref[:, cols] += jnp.sum(dvn * vhat, axis=0, keepdims=True)
            db_ref[:, cols] += jnp.sum(dvn, axis=0, keepdims=True)
            dvhat = dvn * g
            m1 += jnp.sum(dvhat, axis=1, keepdims=True)
            m2 += jnp.sum(dvhat * vhat, axis=1, keepdims=True)
        m1 = m1 * (1.0 / e)
        m2 = m2 * (1.0 / e)
        for h in range(HEADS):
            cols = slice(h * hd, (h + 1) * hd)
            dvg = rstd * (dvn_ref[:, cols] * g_ref[:, cols] - m1 - vh_ref[:, cols] * m2)
            v_pre = p_ref[:, e + h * hd:e + (h + 1) * hd].astype(F32)
            dp_ref[:, e + h * hd:e + (h + 1) * hd] = (dvg * _gelu_grad(v_pre)).astype(BF16)

    vec = pl.BlockSpec((1, e), lambda i: (0, 0))
    mats = pl.BlockSpec((HEADS, CHUNK, CHUNK), lambda i: (0, 0, 0))
    cols1 = pl.BlockSpec((HEADS, CHUNK, 1), lambda i: (0, 0, 0))
    return _call(
        name, body, grid=(t // CHUNK,),
        in_specs=[pl.BlockSpec((CHUNK, e3), lambda i: (i, 0)), pl.BlockSpec((CHUNK, e), lambda i: (i, 0)),
                  vec, vec, mats, mats, cols1],
        out_specs=[pl.BlockSpec((CHUNK, e3), lambda i: (i, 0)), mats, cols1, vec, vec],
        out_shape=[jax.ShapeDtypeStruct((t, e3), BF16), jax.ShapeDtypeStruct((HEADS, CHUNK, CHUNK), F32),
                   jax.ShapeDtypeStruct((HEADS, CHUNK, 1), F32), jax.ShapeDtypeStruct((1, e), F32),
                   jax.ShapeDtypeStruct((1, e), F32)],
        scratch=[pltpu.VMEM((CHUNK, e), F32), pltpu.VMEM((CHUNK, e), F32)],
        sem=("arbitrary",), args=[proj, dy, ln_g, ln_b, w_tril, w_tril_t, bias_t], jobs=jobs)


def _window_sum(xx, steps, backward):
    n = xx.shape[0]
    span = 1
    for _ in range(steps):
        xx = xx + pltpu.roll(xx, (n - span) if backward else span, axis=0)
        span *= 2
    return xx


def _pool_fwd(name, proj, e, seq, ts, tc):
    t = proj.shape[0]
    gw = e // len(WINDOWS)
    per_group = gw // tc
    hb = ts // HALO

    def body(p_ref, halo_ref, o_ref):
        i = pl.program_id(0)
        grp = pl.program_id(1) // per_group
        pos0 = (i * ts) % seq
        x = p_ref[...].astype(F32)
        halo = halo_ref[...].astype(F32) * jnp.where(pos0 == 0, 0.0, 1.0)
        xx = jnp.concatenate([halo, x], axis=0)
        pos = (pos0 + lax.broadcasted_iota(jnp.int32, (ts, 1), 0)).astype(F32)
        for k, w in enumerate(WINDOWS):
            @pl.when(grp == k)
            def _(k=k, w=w):
                win = _window_sum(xx, k + 1, False)[HALO:, :]
                o_ref[...] = (win / jnp.minimum(pos + 1.0, float(w)) - x).astype(BF16)

    return pl.pallas_call(
        body, name=name, grid=(t // ts, e // tc),
        in_specs=[pl.BlockSpec((ts, tc), lambda i, j: (i, j)),
                  pl.BlockSpec((HALO, tc), lambda i, j: (jnp.maximum(i * hb - 1, 0), j))],
        out_specs=pl.BlockSpec((ts, tc), lambda i, j: (i, j)),
        out_shape=jax.ShapeDtypeStruct((t, e), BF16),
        compiler_params=_params("parallel", "parallel"))(proj, proj)


def _pool_bwd(name, dpooled, dproj, seq, ts, tc):
    t, e = dpooled.shape
    gw = e // len(WINDOWS)
    per_group = gw // tc
    hb = ts // HALO
    last = t // HALO - 1

    def body(d_ref, halo_ref, alias_ref, o_ref):
        del alias_ref
        i = pl.program_id(0)
        grp = pl.program_id(1) // per_group
        pos0 = (i * ts) % seq
        d = d_ref[...]
        pos = (pos0 + lax.broadcasted_iota(jnp.int32, (ts, 1), 0)).astype(F32)
        at_end = ((i + 1) * ts) % seq == 0
        for k, w in enumerate(WINDOWS):
            @pl.when(grp == k)
            def _(k=k, w=w):
                r = d / jnp.minimum(pos + 1.0, float(w))
                rh = halo_ref[...] * jnp.where(at_end, 0.0, 1.0 / w)
                win = _window_sum(jnp.concatenate([r, rh], axis=0), k + 1, True)[:ts, :]
                o_ref[...] = (win - d).astype(BF16)

    return pl.pallas_call(
        body, name=name, grid=(t // ts, e // tc),
        in_specs=[pl.BlockSpec((ts, tc), lambda i, j: (i, j)),
                  pl.BlockSpec((HALO, tc), lambda i, j: (jnp.minimum((i + 1) * hb, last), j)),
                  ANY],
        out_specs=pl.BlockSpec((ts, tc), lambda i, j: (i, j)),
        out_shape=jax.ShapeDtypeStruct(dproj.shape, BF16),
        input_output_aliases={2: 0},
        compiler_params=_params("parallel", "parallel"))(dpooled, dpooled, dproj)


def _gate_b_fwd(name, q, proj, b_grp, scale, tr, tc):
    t, e = q.shape
    zoff = e // tc

    def body(q_ref, z_ref, b_ref, s_ref, y_ref):
        z = z_ref[...].astype(F32)
        mixed = (q_ref[...].astype(F32) + b_ref[...]) * s_ref[...]
        y_ref[...] = (mixed * (z * jax.nn.sigmoid(z))).astype(BF16)

    blk = pl.BlockSpec((tr, tc), lambda i, j: (i, j))
    vec = pl.BlockSpec((1, tc), lambda i, j: (0, j))
    return pl.pallas_call(
        body, name=name, grid=(t // tr, e // tc),
        in_specs=[blk, pl.BlockSpec((tr, tc), lambda i, j: (i, zoff + j)), vec, vec],
        out_specs=blk, out_shape=jax.ShapeDtypeStruct((t, e), BF16),
        compiler_params=_params("parallel", "parallel"))(q, proj, b_grp, scale)


def _gate_b_bwd(name, dy, q, proj, b_grp, scale, tr, tc):
    t, e = q.shape
    zoff = e // tc

    def body(dy_ref, q_ref, z_ref, b_ref, s_ref, dq_ref, dz_ref, ds_ref, db_ref):
        z = z_ref[...].astype(F32)
        sig = jax.nn.sigmoid(z)
        dyv = dy_ref[...].astype(F32)
        qb = q_ref[...].astype(F32) + b_ref[...]
        sc = s_ref[...]
        dmixed = dyv * (z * sig)
        dq = dmixed * sc
        dq_ref[...] = dq.astype(BF16)
        dz_ref[...] = (dyv * (qb * sc) * (sig * (1.0 + z * (1.0 - sig)))).astype(BF16)

        @pl.when(pl.program_id(1) == 0)
        def _():
            ds_ref[...] = jnp.zeros_like(ds_ref)
            db_ref[...] = jnp.zeros_like(db_ref)

        ds_ref[...] += jnp.sum(dmixed * qb, axis=0, keepdims=True)
        db_ref[...] += jnp.sum(dq, axis=0, keepdims=True)

    blk = pl.BlockSpec((tr, tc), lambda j, i: (i, j))
    zblk = pl.BlockSpec((tr, tc), lambda j, i: (i, zoff + j))
    vec = pl.BlockSpec((1, tc), lambda j, i: (0, j))
    return pl.pallas_call(
        body, name=name, grid=(e // tc, t // tr),
        in_specs=[blk, blk, zblk, vec, vec], out_specs=[blk, zblk, vec, vec],
        out_shape=[jax.ShapeDtypeStruct((t, e), BF16), jax.ShapeDtypeStruct((t, 2 * e), BF16),
                   jax.ShapeDtypeStruct((1, e), F32), jax.ShapeDtypeStruct((1, e), F32)],
        compiler_params=_params("parallel", "arbitrary"))(dy, q, proj, b_grp, scale)


def _place():
    return lax.axis_index("x"), lax.axis_index("y"), lax.axis_index("c")


def _all_gather(name, blocks):
    n = len(blocks)

    def body(*refs):
        ins, outs = refs[:n], refs[n:2 * n]
        send_sems, recv_sems, local_sems = refs[2 * n:]
        x, y, c = _place()
        me, sibling = (x, y, c), (x, y, 1 - c)
        chips = [(1 - x, y), (x, 1 - y), (1 - x, 1 - y)]

        def copy(a, k, block, to, from_input=False):
            slot = outs[a].at[4 * block[0] + 2 * block[1] + block[2]]
            return pltpu.make_async_remote_copy(
                src_ref=ins[a] if from_input else slot, dst_ref=slot,
                send_sem=send_sems.at[7 * a + k], recv_sem=recv_sems.at[7 * a + k],
                device_id=to, device_id_type=MESH_ID)

        mine = [pltpu.make_async_copy(ins[a], outs[a].at[4 * x + 2 * y + c], local_sems.at[a]) for a in range(n)]
        first = []
        for a in range(n):
            mine[a].start()
            first.append(copy(a, 0, me, sibling, True))
            first += [copy(a, 1 + j, me, (*chip, c), True) for j, chip in enumerate(chips)]
        for cp in first:
            cp.start()
        passed = []
        for j, chip in enumerate(chips):
            for a in range(n):
                copy(a, 1 + j, (*chip, c), me).wait_recv()
                cp = copy(a, 4 + j, (*chip, c), sibling)
                cp.start()
                passed.append(cp)
        for a in range(n):
            copy(a, 0, sibling, me).wait_recv()
            for j, chip in enumerate(chips):
                copy(a, 4 + j, (*chip, 1 - c), me).wait_recv()
        for cp in first + passed:
            cp.wait_send()
        for cp in mine:
            cp.wait()

    return pl.pallas_call(
        body, name=name, in_specs=[ANY] * n, out_specs=[ANY] * n,
        out_shape=[jax.ShapeDtypeStruct((NDEV,) + b.shape, b.dtype) for b in blocks],
        scratch_shapes=[pltpu.SemaphoreType.DMA((7 * n,)), pltpu.SemaphoreType.DMA((7 * n,)),
                        pltpu.SemaphoreType.DMA((n,))],
    )(*blocks)


def _remote(src, dst, sems, to):
    return pltpu.make_async_remote_copy(src_ref=src, dst_ref=dst, send_sem=sems[0], recv_sem=sems[1],
                                        device_id=to, device_id_type=MESH_ID)


def _gather_send_job(blocks):
    n = len(blocks)

    def copies(ins, outs, sem, local_sem):
        x, y, c = _place()
        peers = [(x, y, 1 - c), (1 - x, y, c), (x, 1 - y, c), (1 - x, 1 - y, c)]
        slot = 4 * x + 2 * y + c
        remote = [_remote(ins[a], outs[a].at[slot], sem(4 * a + k), peers[k]) for a in range(n) for k in range(4)]
        local = [pltpu.make_async_copy(ins[a], outs[a].at[slot], local_sem(a)) for a in range(n)]
        return remote, local

    return _Job(blocks, [jax.ShapeDtypeStruct((NDEV,) + b.shape, b.dtype) for b in blocks], {}, 4 * n, n, copies)


def _gather_pass_job(gathered):
    n = len(gathered)

    def copies(ins, outs, sem, local_sem):
        del ins, local_sem
        x, y, c = _place()
        slots = [4 * (1 - x) + 2 * y + c, 4 * x + 2 * (1 - y) + c, 4 * (1 - x) + 2 * (1 - y) + c]
        remote = [_remote(outs[a].at[slots[j]], outs[a].at[slots[j]], sem(3 * a + j), (x, y, 1 - c))
                  for a in range(n) for j in range(3)]
        return remote, []

    return _Job(gathered, [jax.ShapeDtypeStruct(g.shape, g.dtype) for g in gathered], {a: a for a in range(n)},
                3 * n, 0, copies)


def _to_sibling_job(grads, by_class):
    n = len(grads)

    def copies(ins, outs, sem, local_sem):
        del local_sem
        x, y, c = _place()
        remote = [_remote(ins[a].at[q if by_class[a] else 2 * q + 1 - c], outs[a].at[q], sem(4 * a + q), (x, y, 1 - c))
                  for a in range(n) for q in range(4)]
        return remote, []

    return _Job(grads, [jax.ShapeDtypeStruct((4,) + g.shape[1:], g.dtype) for g in grads], {}, 4 * n, 0, copies)


def _to_chips_job(sums):
    n = len(sums)

    def copies(ins, outs, sem, local_sem):
        del local_sem
        x, y, c = _place()
        chips = [(1 - x, y, c), (x, 1 - y, c), (1 - x, 1 - y, c)]
        remote = [_remote(ins[a].at[j], outs[a].at[j], sem(3 * a + j), chips[j]) for a in range(n) for j in range(3)]
        return remote, []

    return _Job(sums, [jax.ShapeDtypeStruct(s.shape, s.dtype) for s in sums], {}, 3 * n, 0, copies)


def _adamw(w, g, m, v):
    m = ADAM_B1 * m + (1.0 - ADAM_B1) * g
    v = ADAM_B2 * v + (1.0 - ADAM_B2) * (g * g)
    m_hat = m / (1.0 - ADAM_B1 ** ADAM_STEP)
    v_hat = v / (1.0 - ADAM_B2 ** ADAM_STEP)
    delta = -ADAM_LR * (m_hat / (jnp.sqrt(v_hat) + ADAM_EPS) + ADAM_WD * w)
    return delta, m, v


def _row_tile(rows, cols):
    return _tile(rows, max(F32_SUBLANES, (3 * 128 * 1024) // cols), F32_SUBLANES)


def _chip_sums(name, ids, grad, from_sibling, by_class=False):
    _, r, c = grad.shape
    tr = _row_tile(r, c)

    def body(ids_ref, g_ref, s_ref, o_ref):
        del ids_ref
        o_ref[...] = (g_ref[...].astype(F32) + s_ref[...].astype(F32)).astype(o_ref.dtype)

    return pl.pallas_call(
        body, name=name,
        grid_spec=pltpu.PrefetchScalarGridSpec(
            num_scalar_prefetch=1, grid=(3, r // tr),
            in_specs=[pl.BlockSpec((None, tr, c), (lambda j, i, ids: (ids[j], i, 0)) if by_class
                                   else (lambda j, i, ids: (2 * ids[j] + ids[3], i, 0))),
                      pl.BlockSpec((None, tr, c), lambda j, i, ids: (ids[j], i, 0))],
            out_specs=pl.BlockSpec((None, tr, c), lambda j, i, ids: (j, i, 0))),
        out_shape=jax.ShapeDtypeStruct((3, r, c), grad.dtype),
        compiler_params=_params("parallel", "parallel"))(ids, grad, from_sibling)


def _reduce_adamw(name, ids, grad, from_sibling, from_chips, w, m, v, by_class=False):
    _, r, c = grad.shape
    tr = _row_tile(r, c)

    def body(ids_ref, g_ref, s_ref, c_ref, w_ref, m_ref, v_ref, og_ref, od_ref, om_ref, ov_ref):
        del ids_ref
        g = g_ref[...].astype(F32) + s_ref[...].astype(F32)
        for j in range(3):
            g = g + c_ref[j].astype(F32)
        og_ref[...] = g
        od_ref[...], om_ref[...], ov_ref[...] = _adamw(w_ref[...], g, m_ref[...], v_ref[...])

    blk = pl.BlockSpec((tr, c), lambda i, ids: (i, 0))
    out = jax.ShapeDtypeStruct((r, c), F32)
    return pl.pallas_call(
        body, name=name,
        grid_spec=pltpu.PrefetchScalarGridSpec(
            num_scalar_prefetch=1, grid=(r // tr,),
            in_specs=[pl.BlockSpec((None, tr, c), lambda i, ids: (ids[4 if by_class else 5], i, 0)),
                      pl.BlockSpec((None, tr, c), lambda i, ids: (ids[4], i, 0)),
                      pl.BlockSpec((3, tr, c), lambda i, ids: (0, i, 0)), blk, blk, blk],
            out_specs=[blk, blk, blk, blk]),
        out_shape=[out, out, out, out],
        compiler_params=_params("parallel"))(ids, grad, from_sibling, from_chips, w, m, v)


def _small_adamw(name, gathered, mask, w, m, v):
    _, r, c = gathered.shape

    def body(g_ref, k_ref, w_ref, m_ref, v_ref, og_ref, od_ref, om_ref, ov_ref):
        g = g_ref[0]
        for dev in range(1, NDEV):
            g = g + g_ref[dev]
        g = g * k_ref[...]
        og_ref[...] = g
        od_ref[...], om_ref[...], ov_ref[...] = _adamw(w_ref[...], g, m_ref[...], v_ref[...])

    out = jax.ShapeDtypeStruct((r, c), F32)
    return pl.pallas_call(
        body, name=name, out_shape=[out, out, out, out],
        compiler_params=pltpu.CompilerParams(vmem_limit_bytes=VMEM_LIMIT))(gathered, mask, w, m, v)


def kernel(x, pre_norm, post_norm, a_w_in, a_ln_g, a_ln_b, a_w_s, a_b_s, a_w_out, b_w_in, b_w_grp, b_b_grp, b_scale, b_w_out, loss_target, m_pre_norm, m_post_norm, m_a_w_in, m_a_ln_g, m_a_ln_b, m_a_w_s, m_a_b_s, m_a_w_out, m_b_w_in, m_b_w_grp, m_b_b_grp, m_b_scale, m_b_w_out, v_pre_norm, v_post_norm, v_a_w_in, v_a_ln_g, v_a_ln_b, v_a_w_s, v_a_b_s, v_a_w_out, v_b_w_in, v_b_w_grp, v_b_b_grp, v_b_scale, v_b_w_out):
    bl, seq, d = x.shape
    t = bl * seq
    e = a_w_out.shape[1] * NDEV
    groups = len(WINDOWS)
    gw = e // groups
    c3, c2 = a_w_in.shape[2], b_w_in.shape[2]
    assert e // HEADS % LANES == 0 and seq % CHUNK == 0 and (2 * d // LANES) % F32_SUBLANES == 0

    bm = _tile(t, 1024)
    bn = 1024
    bd = _tile(d, 1024)
    be = _tile(e, 1024)
    bgw = _tile(gw, 1024)
    bk = _tile(e, 4096)
    tr = _tile(t, 128, F32_SUBLANES)
    ts = _tile(seq, 512, HALO)
    tc = _tile(gw, 1024)
    tre = _tile(t, 256, HALO)

    xc, yc, cc = _place()
    my_chip = 2 * xc + yc
    ids = jnp.stack([2 * (1 - xc) + yc, 2 * xc + (1 - yc), 2 * (1 - xc) + (1 - yc), cc, my_chip,
                     2 * my_chip + cc, 1 - cc, 0]).astype(jnp.int32)
    results = {}

    def finish(wname, g, from_sibling, from_chips, w, m, v, by_class=False):
        shape2d = g.shape[1:]
        outs = _reduce_adamw("adamw_" + wname, ids, g, from_sibling, from_chips, w.reshape(shape2d),
                             m.reshape(shape2d), v.reshape(shape2d), by_class=by_class)
        results[wname] = [o.reshape(w.shape) for o in outs]

    (wa_in,) = _all_gather("gather_a_in", [a_w_in[0].astype(BF16)])
    w_tril = jnp.tril(a_w_s[0]).astype(BF16)
    w_tril_t = jnp.swapaxes(w_tril, 1, 2)
    bias_t = a_b_s[0][:, :, None]
    x2d = x.reshape(t, d)
    tgt = loss_target.reshape(t, d)

    h0 = _rms_fwd("pre_norm_a", x2d, pre_norm[0:1], tr)
    proj_a, sent = _mm_in_proj("a_in_proj", h0, wa_in, bm, bn, jobs=[
        _gather_send_job([a_w_out[0].astype(BF16), b_w_in[0].astype(BF16)])])
    y_a, (wa_out, wb_in) = _gate_a_fwd("a_gate", proj_a, a_ln_g, a_ln_b, w_tril, bias_t, jobs=[_gather_pass_job(sent)])
    wa_out = wa_out.reshape(e, d)
    m_a, sent = _mm_out_proj("a_out_proj", y_a, wa_out, bm, bd, bk, jobs=[
        _gather_send_job([b_w_grp[0].astype(BF16), b_w_out[0].astype(BF16), b_b_grp[0], b_scale])])
    x1, h1 = _post_pre("post_a_pre_b", x2d, m_a, post_norm[0:1], pre_norm[1:2], tr)
    proj_b, (wb_grp, wb_out, bgrp_g, scale_g) = _mm_in_proj("b_in_proj", h1, wb_in, bm, bn, jobs=[_gather_pass_job(sent)])
    wb_out = wb_out.reshape(e, d)
    wb_grp = jnp.transpose(wb_grp, (1, 0, 2, 3)).reshape(groups, gw, gw)
    bgrp = jnp.transpose(bgrp_g, (1, 0, 2)).reshape(1, e)
    scale = scale_g.reshape(1, e)
    pooled = _pool_fwd("b_pool", proj_b, e, seq, ts, tc)
    q, _ = _mm_grouped("b_group_proj", pooled, wb_grp, "nn", bm, bgw, BF16)
    y_b = _gate_b_fwd("b_gate", q, proj_b, bgrp, scale, tre, tc)
    m_b, _ = _mm_out_proj("b_out_proj", y_b, wb_out, bm, bd, bk)
    sq, dres, dm_b, dg_post1 = _loss_head("loss_head", x1, m_b, tgt, post_norm[1:2], tr)
    loss = lax.psum(0.5 * jnp.sum(sq) / d, ("x", "y", "c"))

    dwb_out, _ = _mm_dw("b_out_dw", y_b, dm_b, be, bd)
    g_b_out = dwb_out.reshape(NDEV, e // NDEV, d)
    dy_b, (fs_b_out,) = _mm_dact_out("b_out_dact", dm_b, wb_out, bm, be, jobs=[_to_sibling_job([g_b_out], [False])])
    cs_b_out = _chip_sums("chip_sum_b_w_out", ids, g_b_out, fs_b_out)
    dq, dproj_b, dscale, dbgrp = _gate_b_bwd("b_gate_bwd", dy_b, q, proj_b, bgrp, scale, tre, tc)
    dwb_grp, _ = _mm_dw_grouped("b_group_dw", pooled, dq, groups, bgw, bgw)
    grp_names = ["b_w_grp", "b_b_grp", "b_scale"]
    g_grp = [jnp.transpose(dwb_grp.reshape(groups, NDEV, gw // NDEV, gw), (1, 0, 2, 3)).reshape(NDEV, groups * gw // NDEV, gw),
             jnp.transpose(dbgrp.reshape(groups, NDEV, gw // NDEV), (1, 0, 2)),
             dscale.reshape(NDEV, 1, e // NDEV)]
    dpooled, fs_grp = _mm_grouped("b_group_dact", dq, wb_grp, "nt", bm, bgw, F32, jobs=[_to_sibling_job(g_grp, [False] * 3)])
    cs_grp = [_chip_sums("chip_sum_" + n, ids, g, fs) for n, g, fs in zip(grp_names, g_grp, fs_grp)]
    dproj_b = _pool_bwd("b_pool_bwd", dpooled, dproj_b, seq, ts, tc)
    g_b_in, fc = _mm_dw_sharded("b_in_dw", h1, dproj_b, c2, bd, bn, jobs=[_to_chips_job([cs_b_out] + cs_grp)])
    finish("b_w_out", g_b_out, fs_b_out, fc[0], b_w_out, m_b_w_out, v_b_w_out)
    finish("b_w_grp", g_grp[0], fs_grp[0], fc[1], b_w_grp, m_b_w_grp, v_b_w_grp)
    finish("b_b_grp", g_grp[1], fs_grp[1], fc[2], b_b_grp, m_b_b_grp, v_b_b_grp)
    finish("b_scale", g_grp[2], fs_grp[2], fc[3], b_scale, m_b_scale, v_b_scale)
    dh1, (fs_b_in,) = _mm_dact_in("b_in_dact", dproj_b, wb_in, bm, bd, jobs=[_to_sibling_job([g_b_in], [False])])
    cs_b_in = _chip_sums("chip_sum_b_w_in", ids, g_b_in, fs_b_in)
    dres, dm_a, dg_pre1, dg_post0 = _norms_bwd_mid("norms_bwd_mid", dres, dh1, x1, pre_norm[1:2], m_a, post_norm[0:1], tr)

    dwa_out, _ = _mm_dw("a_out_dw", y_a, dm_a, be, bd)
    g_a_out = dwa_out.reshape(NDEV, e // NDEV, d)
    dy_a, (fs_a_out,) = _mm_dact_out("a_out_dact", dm_a, wa_out, bm, be, jobs=[_to_sibling_job([g_a_out], [False])])
    cs_a_out = _chip_sums("chip_sum_a_w_out", ids, g_a_out, fs_a_out)
    (dproj_a, dws, dbs, dlng, dlnb), (fc_b_in,) = _gate_a_bwd(
        "a_gate_bwd", proj_a, dy_a, a_ln_g, a_ln_b, w_tril, w_tril_t, bias_t, jobs=[_to_chips_job([cs_b_in])])
    finish("b_w_in", g_b_in, fs_b_in, fc_b_in, b_w_in, m_b_w_in, v_b_w_in)
    g_a_in_sib, (fc_a_out,) = _mm_dw_class("a_in_dw_sibling", ids, 6, h0, dproj_a, c3, bd, bn, jobs=[_to_chips_job([cs_a_out])])
    finish("a_w_out", g_a_out, fs_a_out, fc_a_out, a_w_out, m_a_w_out, v_a_w_out)
    g_a_in, (fs_a_in,) = _mm_dw_class("a_in_dw_own", ids, 3, h0, dproj_a, c3, bd, bn, jobs=[_to_sibling_job([g_a_in_sib], [True])])
    cs_a_in = _chip_sums("chip_sum_a_w_in", ids, g_a_in, fs_a_in, by_class=True)
    dh0, (fc_a_in,) = _mm_dact_in("a_in_dact", dproj_a, wa_in, bm, bd, jobs=[_to_chips_job([cs_a_in])])
    finish("a_w_in", g_a_in, fs_a_in, fc_a_in, a_w_in, m_a_w_in, v_a_w_in, by_class=True)
    grad_x, dg_pre0 = _norm_bwd_first("norm_bwd_first", dres, dh0, x2d, pre_norm[0:1], tr)

    small = [
        ("pre_norm", jnp.concatenate([dg_pre0, dg_pre1], axis=0), pre_norm, m_pre_norm, v_pre_norm),
        ("post_norm", jnp.concatenate([dg_post0, dg_post1], axis=0), post_norm, m_post_norm, v_post_norm),
        ("a_ln_g", dlng, a_ln_g, m_a_ln_g, v_a_ln_g),
        ("a_ln_b", dlnb, a_ln_b, m_a_ln_b, v_a_ln_b),
        ("a_w_s", dws, a_w_s, m_a_w_s, v_a_w_s),
        ("a_b_s", dbs, a_b_s, m_a_b_s, v_a_b_s),
    ]

    def pack(arrays):
        return jnp.concatenate([a.reshape(-1, LANES) for a in arrays], axis=0)

    tril_mask = jnp.broadcast_to(jnp.tril(jnp.ones((CHUNK, CHUNK), F32)), a_w_s.shape)
    mask = pack([tril_mask if s[0] == "a_w_s" else jnp.ones(s[2].shape, F32) for s in small])
    (gathered,) = _all_gather("gather_small_grads", [pack([s[1] for s in small])])
    packed = _small_adamw("adamw_small", gathered, mask, pack([s[2] for s in small]),
                          pack([s[3] for s in small]), pack([s[4] for s in small]))
    row = 0
    for wname, _, w, _, _ in small:
        rows = w.size // LANES
        results[wname] = [p[row:row + rows].reshape(w.shape) for p in packed]
        row += rows

    order = ["pre_norm", "post_norm", "a_w_in", "a_ln_g", "a_ln_b", "a_w_s", "a_b_s", "a_w_out",
             "b_w_in", "b_w_grp", "b_b_grp", "b_scale", "b_w_out"]
    out = [loss, grad_x.reshape(x.shape)]
    for kind in range(4):
        out += [results[wname][kind] for wname in order]
    return tuple(out)
```

```python
import functools
import math

import jax
import jax.numpy as jnp
from jax import lax
from jax.experimental import pallas as pl
from jax.experimental.pallas import tpu as pltpu

NDEV = 8
CHUNK = 128
HEADS = 16
WINDOWS = (2, 4, 8, 16)
HALO = 16
NORM_EPS = 1e-6
ADAM_LR, ADAM_B1, ADAM_B2, ADAM_EPS, ADAM_WD, ADAM_STEP = 0.001, 0.9, 0.999, 1e-08, 0.01, 10
LANES = 128
F32_SUBLANES = 8
VMEM_LIMIT = 60 * 1024 * 1024
BF16 = jnp.bfloat16
F32 = jnp.float32
MESH_ID = pl.DeviceIdType.MESH
ANY = pl.BlockSpec(memory_space=pl.ANY)


def _tile(dim, pref, mult=LANES):
    t = (min(pref, dim) // mult) * mult
    while t >= mult:
        if dim % t == 0:
            return t
        t -= mult
    return dim


def _params(*sem):
    return pltpu.CompilerParams(dimension_semantics=sem, vmem_limit_bytes=VMEM_LIMIT)


def _gelu(x):
    return 0.5 * x * (1.0 + lax.erf(x * (1.0 / math.sqrt(2.0))))


def _gelu_grad(x):
    return 0.5 * (1.0 + lax.erf(x * (1.0 / math.sqrt(2.0)))) + x * jnp.exp(-0.5 * x * x) * (1.0 / math.sqrt(2.0 * math.pi))


class _Job:
    def __init__(self, arrays, out_shapes, aliases, n_remote, n_local, copies, starts=(0.0,)):
        self.arrays, self.out_shapes, self.aliases = list(arrays), list(out_shapes), dict(aliases)
        self.n_remote, self.n_local, self.copies, self.starts = n_remote, n_local, copies, tuple(starts)


def _call(name, body, *, grid, in_specs, out_specs, out_shape, args, sem, scratch=(), jobs=(), prefetch=None):
    n_in, n_out, n_scr = len(in_specs), len(out_specs), len(scratch)
    job_in = [a for job in jobs for a in job.arrays]
    job_out = [s for job in jobs for s in job.out_shapes]
    aliases = {}
    i_off, o_off = n_in, n_out
    for job in jobs:
        for i, o in job.aliases.items():
            aliases[i_off + i] = o_off + o
        i_off += len(job.arrays)
        o_off += len(job.out_shapes)
    n_remote = sum(job.n_remote for job in jobs)
    n_local = sum(job.n_local for job in jobs)
    sems = [pltpu.SemaphoreType.DMA((max(n_remote, 1),)), pltpu.SemaphoreType.DMA((max(n_remote, 1),)),
            pltpu.SemaphoreType.DMA((max(n_local, 1),))] if jobs else []
    steps = math.prod(grid)

    def wrapped(*refs):
        if prefetch is not None:
            refs = refs[1:]
        ins, jins = refs[:n_in], refs[n_in:n_in + len(job_in)]
        rest = refs[n_in + len(job_in):]
        outs, jouts = rest[:n_out], rest[n_out:n_out + len(job_out)]
        scr = rest[n_out + len(job_out):n_out + len(job_out) + n_scr]

        events = {}
        final = []
        if jobs:
            send_sems, recv_sems, local_sems = rest[-3:]
            i0 = o0 = r0 = l0 = 0
            for job in jobs:
                phases = job.copies(
                    jins[i0:i0 + len(job.arrays)], jouts[o0:o0 + len(job.out_shapes)],
                    lambda k, r0=r0: (send_sems.at[r0 + k], recv_sems.at[r0 + k]),
                    lambda k, l0=l0: local_sems.at[l0 + k])
                assert len(phases) == len(job.starts)
                for p, copies in enumerate(phases):
                    at = min(steps - 1, int(steps * job.starts[p]))
                    if p > 0:
                        events.setdefault(at, []).append(("wait", phases[p - 1]))
                    events.setdefault(at, []).append(("start", copies))
                final += phases[-1]
                i0 += len(job.arrays)
                o0 += len(job.out_shapes)
                r0 += job.n_remote
                l0 += job.n_local
            step = 0
            for i in range(len(grid)):
                step = step * grid[i] + pl.program_id(i)

        for at in sorted(events):
            @pl.when(step == at)
            def _(at=at):
                for action, copies in events[at]:
                    for cp in copies:
                        cp.start() if action == "start" else cp.wait()

        body(*ins, *outs, *scr)

        if jobs:
            @pl.when(step == steps - 1)
            def _():
                for cp in final:
                    cp.wait()

    all_in = list(in_specs) + [ANY] * len(job_in)
    all_out = list(out_specs) + [ANY] * len(job_out)
    shapes = list(out_shape) + job_out
    operands = list(args) + job_in
    if prefetch is None:
        res = pl.pallas_call(
            wrapped, name=name, grid=grid, in_specs=all_in, out_specs=all_out, out_shape=shapes,
            scratch_shapes=list(scratch) + sems, input_output_aliases=aliases, compiler_params=_params(*sem))(*operands)
    else:
        assert not aliases
        res = pl.pallas_call(
            wrapped, name=name,
            grid_spec=pltpu.PrefetchScalarGridSpec(
                num_scalar_prefetch=1, grid=grid, in_specs=all_in, out_specs=all_out,
                scratch_shapes=list(scratch) + sems),
            out_shape=shapes, compiler_params=_params(*sem))(prefetch, *operands)
    return list(res[:n_out]), list(res[n_out:])


_DOT_DIMS = {"nn": (((1,), (0,)), ((), ())), "nt": (((1,), (1,)), ((), ())), "tn": (((0,), (0,)), ((), ()))}


def _matmul(name, a, b, *, mode, grid, a_spec, b_spec, o_spec, out_shape, nk, jobs=(), prefetch=None):
    dims = _DOT_DIMS[mode]
    acc_shape = tuple(d for d in o_spec.block_shape if d is not None)

    if nk == 1:
        def body(a_ref, b_ref, o_ref):
            o_ref[...] = lax.dot_general(a_ref[...], b_ref[...], dims, preferred_element_type=F32).astype(o_ref.dtype)
        scratch = []
        sem = ("parallel",) * len(grid) if not jobs else ("arbitrary",) * len(grid)
    else:
        def body(a_ref, b_ref, o_ref, acc_ref):
            k = pl.program_id(len(grid) - 1)
            part = lax.dot_general(a_ref[...], b_ref[...], dims, preferred_element_type=F32)

            @pl.when(k == 0)
            def _():
                acc_ref[...] = part

            @pl.when(jnp.logical_and(k > 0, k < nk - 1))
            def _():
                acc_ref[...] += part

            @pl.when(k == nk - 1)
            def _():
                o_ref[...] = (acc_ref[...] + part).astype(o_ref.dtype)
        scratch = [pltpu.VMEM(acc_shape, F32)]
        sem = (("parallel",) * (len(grid) - 1) if not jobs else ("arbitrary",) * (len(grid) - 1)) + ("arbitrary",)

    outs, job_outs = _call(name, body, grid=grid, in_specs=[a_spec, b_spec], out_specs=[o_spec], out_shape=[out_shape],
                           args=[a, b], sem=sem, scratch=scratch, jobs=jobs, prefetch=prefetch)
    return outs[0], job_outs


def _mm_in_proj(name, h, w_g, bm, bn, jobs=()):
    t, d = h.shape
    _, _, c = w_g.shape
    bn = _tile(c, bn)
    nb = c // bn
    return _matmul(
        name, h, w_g, mode="nn", grid=(t // bm, NDEV * nb), nk=1, jobs=jobs,
        a_spec=pl.BlockSpec((bm, d), lambda i, j: (i, 0)),
        b_spec=pl.BlockSpec((None, d, bn), lambda i, j: (j // nb, 0, j % nb)),
        o_spec=pl.BlockSpec((bm, bn), lambda i, j: (i, j)),
        out_shape=jax.ShapeDtypeStruct((t, NDEV * c), BF16))


def _mm_out_proj(name, y, w, bm, bn, bk, jobs=()):
    t, e = y.shape
    _, d = w.shape
    nk = e // bk
    return _matmul(
        name, y, w, mode="nn", grid=(t // bm, d // bn, nk), nk=nk, jobs=jobs,
        a_spec=pl.BlockSpec((bm, bk), lambda i, j, k: (i, k)),
        b_spec=pl.BlockSpec((bk, bn), lambda i, j, k: (k, j)),
        o_spec=pl.BlockSpec((bm, bn), lambda i, j, k: (i, j)),
        out_shape=jax.ShapeDtypeStruct((t, d), F32))


def _mm_grouped(name, a, w, mode, bm, bn, out_dtype, jobs=()):
    t, e = a.shape
    g_n, gw, _ = w.shape
    nb = gw // bn
    if mode == "nn":
        b_spec = pl.BlockSpec((None, gw, bn), lambda g, i, j: (g, 0, j))
    else:
        b_spec = pl.BlockSpec((None, bn, gw), lambda g, i, j: (g, j, 0))
    return _matmul(
        name, a, w, mode=mode, grid=(g_n, t // bm, nb), nk=1, jobs=jobs,
        a_spec=pl.BlockSpec((bm, gw), lambda g, i, j: (i, g)),
        b_spec=b_spec,
        o_spec=pl.BlockSpec((bm, bn), lambda g, i, j: (i, g * nb + j)),
        out_shape=jax.ShapeDtypeStruct((t, e), out_dtype))


def _mm_dact_out(name, dm, w, bm, bn, jobs=()):
    t, d = dm.shape
    e, _ = w.shape
    return _matmul(
        name, dm, w, mode="nt", grid=(t // bm, e // bn), nk=1, jobs=jobs,
        a_spec=pl.BlockSpec((bm, d), lambda i, j: (i, 0)),
        b_spec=pl.BlockSpec((bn, d), lambda i, j: (j, 0)),
        o_spec=pl.BlockSpec((bm, bn), lambda i, j: (i, j)),
        out_shape=jax.ShapeDtypeStruct((t, e), BF16))


def _mm_dact_in(name, dproj, w_g, bm, bn, jobs=()):
    t, _ = dproj.shape
    _, d, c = w_g.shape
    return _matmul(
        name, dproj, w_g, mode="nt", grid=(t // bm, d // bn, NDEV), nk=NDEV, jobs=jobs,
        a_spec=pl.BlockSpec((bm, c), lambda i, j, k: (i, k)),
        b_spec=pl.BlockSpec((None, bn, c), lambda i, j, k: (k, j, 0)),
        o_spec=pl.BlockSpec((bm, bn), lambda i, j, k: (i, j)),
        out_shape=jax.ShapeDtypeStruct((t, d), F32))


def _mm_dw(name, a, b, bm, bn, jobs=()):
    t, m = a.shape
    _, n = b.shape
    return _matmul(
        name, a, b, mode="tn", grid=(m // bm, n // bn), nk=1, jobs=jobs,
        a_spec=pl.BlockSpec((t, bm), lambda i, j: (0, i)),
        b_spec=pl.BlockSpec((t, bn), lambda i, j: (0, j)),
        o_spec=pl.BlockSpec((bm, bn), lambda i, j: (i, j)),
        out_shape=jax.ShapeDtypeStruct((m, n), BF16))


def _mm_dw_sharded(name, a, b, c, bm, bn, jobs=()):
    t, m = a.shape
    bn = _tile(c, bn)
    nb = c // bn
    return _matmul(
        name, a, b, mode="tn", grid=(m // bm, NDEV * nb), nk=1, jobs=jobs,
        a_spec=pl.BlockSpec((t, bm), lambda i, j: (0, i)),
        b_spec=pl.BlockSpec((t, bn), lambda i, j: (0, j)),
        o_spec=pl.BlockSpec((None, bm, bn), lambda i, j: (j // nb, i, j % nb)),
        out_shape=jax.ShapeDtypeStruct((NDEV, m, c), BF16))


def _mm_dw_class(name, ids, cls_slot, a, b, c, bm, bn, jobs=()):
    t, m = a.shape
    bn = _tile(c, bn)
    nb = c // bn
    return _matmul(
        name, a, b, mode="tn", grid=(m // bm, 4 * nb), nk=1, jobs=jobs, prefetch=ids,
        a_spec=pl.BlockSpec((t, bm), lambda i, j, ids: (0, i)),
        b_spec=pl.BlockSpec((t, bn), lambda i, j, ids: (0, (2 * (j // nb) + ids[cls_slot]) * nb + j % nb)),
        o_spec=pl.BlockSpec((None, bm, bn), lambda i, j, ids: (j // nb, i, j % nb)),
        out_shape=jax.ShapeDtypeStruct((4, m, c), BF16))


def _mm_dw_grouped(name, a, b, g_n, bm, bn, jobs=()):
    t, e = a.shape
    gw = e // g_n
    nbm, nbn = gw // bm, gw // bn
    return _matmul(
        name, a, b, mode="tn", grid=(g_n, nbm, nbn), nk=1, jobs=jobs,
        a_spec=pl.BlockSpec((t, bm), lambda g, i, j: (0, g * nbm + i)),
        b_spec=pl.BlockSpec((t, bn), lambda g, i, j: (0, g * nbn + j)),
        o_spec=pl.BlockSpec((None, bm, bn), lambda g, i, j: (g, i, j)),
        out_shape=jax.ShapeDtypeStruct((g_n, gw, gw), BF16))


def _rms(x):
    return lax.rsqrt(jnp.mean(x * x, axis=-1, keepdims=True) + NORM_EPS)


def _rms_bwd(dy, x, g):
    r = _rms(x)
    xhat = x * r
    dxhat = dy * g
    dx = r * (dxhat - xhat * jnp.mean(dxhat * xhat, axis=-1, keepdims=True))
    return dx, dy * xhat


def _rms_fwd(name, x, g, tr):
    t, d = x.shape

    def body(x_ref, g_ref, h_ref):
        xv = x_ref[...]
        h_ref[...] = (xv * _rms(xv) * g_ref[...]).astype(BF16)

    row = pl.BlockSpec((tr, d), lambda i: (i, 0))
    vec = pl.BlockSpec((1, d), lambda i: (0, 0))
    return pl.pallas_call(
        body, name=name, grid=(t // tr,), in_specs=[row, vec], out_specs=row,
        out_shape=jax.ShapeDtypeStruct((t, d), BF16), compiler_params=_params("parallel"))(x, g)


def _post_pre(name, x, m, g_post, g_pre, tr):
    t, d = x.shape

    def body(x_ref, m_ref, gp_ref, gn_ref, x1_ref, h_ref):
        mv = m_ref[...]
        x1 = x_ref[...] + mv * _rms(mv) * gp_ref[...]
        x1_ref[...] = x1
        h_ref[...] = (x1 * _rms(x1) * gn_ref[...]).astype(BF16)

    row = pl.BlockSpec((tr, d), lambda i: (i, 0))
    vec = pl.BlockSpec((1, d), lambda i: (0, 0))
    return pl.pallas_call(
        body, name=name, grid=(t // tr,), in_specs=[row, row, vec, vec], out_specs=[row, row],
        out_shape=[jax.ShapeDtypeStruct((t, d), F32), jax.ShapeDtypeStruct((t, d), BF16)],
        compiler_params=_params("parallel"))(x, m, g_post, g_pre)


def _loss_head(name, x1, m, tgt, g_post, tr):
    t, d = x1.shape

    def body(x1_ref, m_ref, t_ref, g_ref, sq_ref, dres_ref, dm_ref, dg_ref):
        mv = m_ref[...]
        g = g_ref[...]
        diff = x1_ref[...] + mv * _rms(mv) * g - t_ref[...]
        dout = diff * (1.0 / d)
        dm, dg_rows = _rms_bwd(dout, mv, g)
        dres_ref[...] = dout
        dm_ref[...] = dm.astype(BF16)

        @pl.when(pl.program_id(0) == 0)
        def _():
            sq_ref[...] = jnp.zeros_like(sq_ref)
            dg_ref[...] = jnp.zeros_like(dg_ref)

        sq_ref[...] += jnp.sum(diff * diff, axis=0, keepdims=True)
        dg_ref[...] += jnp.sum(dg_rows, axis=0, keepdims=True)

    row = pl.BlockSpec((tr, d), lambda i: (i, 0))
    vec = pl.BlockSpec((1, d), lambda i: (0, 0))
    return pl.pallas_call(
        body, name=name, grid=(t // tr,), in_specs=[row, row, row, vec], out_specs=[vec, row, row, vec],
        out_shape=[jax.ShapeDtypeStruct((1, d), F32), jax.ShapeDtypeStruct((t, d), F32),
                   jax.ShapeDtypeStruct((t, d), BF16), jax.ShapeDtypeStruct((1, d), F32)],
        compiler_params=_params("arbitrary"))(x1, m, tgt, g_post)


def _norms_bwd_mid(name, dres, dh, x1, g_pre, m, g_post, tr):
    t, d = x1.shape

    def body(dres_ref, dh_ref, x1_ref, gpre_ref, m_ref, gpost_ref, out_ref, dm_ref, dgpre_ref, dgpost_ref):
        dx, dgpre_rows = _rms_bwd(dh_ref[...], x1_ref[...], gpre_ref[...])
        dres2 = dres_ref[...] + dx
        out_ref[...] = dres2
        dm, dgpost_rows = _rms_bwd(dres2, m_ref[...], gpost_ref[...])
        dm_ref[...] = dm.astype(BF16)

        @pl.when(pl.program_id(0) == 0)
        def _():
            dgpre_ref[...] = jnp.zeros_like(dgpre_ref)
            dgpost_ref[...] = jnp.zeros_like(dgpost_ref)

        dgpre_ref[...] += jnp.sum(dgpre_rows, axis=0, keepdims=True)
        dgpost_ref[...] += jnp.sum(dgpost_rows, axis=0, keepdims=True)

    row = pl.BlockSpec((tr, d), lambda i: (i, 0))
    vec = pl.BlockSpec((1, d), lambda i: (0, 0))
    return pl.pallas_call(
        body, name=name, grid=(t // tr,), in_specs=[row, row, row, vec, row, vec], out_specs=[row, row, vec, vec],
        out_shape=[jax.ShapeDtypeStruct((t, d), F32), jax.ShapeDtypeStruct((t, d), BF16),
                   jax.ShapeDtypeStruct((1, d), F32), jax.ShapeDtypeStruct((1, d), F32)],
        compiler_params=_params("arbitrary"))(dres, dh, x1, g_pre, m, g_post)


def _norm_bwd_first(name, dres, dh, x, g_pre, tr):
    t, d = x.shape

    def body(dres_ref, dh_ref, x_ref, g_ref, out_ref, dg_ref):
        dx, dg_rows = _rms_bwd(dh_ref[...], x_ref[...], g_ref[...])
        out_ref[...] = dres_ref[...] + dx

        @pl.when(pl.program_id(0) == 0)
        def _():
            dg_ref[...] = jnp.zeros_like(dg_ref)

        dg_ref[...] += jnp.sum(dg_rows, axis=0, keepdims=True)

    row = pl.BlockSpec((tr, d), lambda i: (i, 0))
    vec = pl.BlockSpec((1, d), lambda i: (0, 0))
    return pl.pallas_call(
        body, name=name, grid=(t // tr,), in_specs=[row, row, row, vec], out_specs=[row, vec],
        out_shape=[jax.ShapeDtypeStruct((t, d), F32), jax.ShapeDtypeStruct((1, d), F32)],
        compiler_params=_params("arbitrary"))(dres, dh, x, g_pre)


def _gate_a_fwd(name, proj, ln_g, ln_b, w_tril, bias_t, jobs=()):
    t, e3 = proj.shape
    e = e3 // 3
    hd = e // HEADS

    def body(p_ref, g_ref, b_ref, w_ref, bs_ref, y_ref, vg_ref):
        s1 = jnp.zeros((CHUNK, 1), F32)
        s2 = jnp.zeros((CHUNK, 1), F32)
        for h in range(HEADS):
            cols = slice(h * hd, (h + 1) * hd)
            vg = _gelu(p_ref[:, e + h * hd:e + (h + 1) * hd].astype(F32))
            vg_ref[:, cols] = vg
            s1 += jnp.sum(vg, axis=1, keepdims=True)
            s2 += jnp.sum(vg * vg, axis=1, keepdims=True)
        mu = s1 * (1.0 / e)
        rstd = lax.rsqrt(s2 * (1.0 / e) - mu * mu + NORM_EPS)
        for h in range(HEADS):
            cols = slice(h * hd, (h + 1) * hd)
            vn = ((vg_ref[:, cols] - mu) * rstd * g_ref[:, cols] + b_ref[:, cols]).astype(BF16)
            sv = jnp.dot(w_ref[h], vn, preferred_element_type=F32) + bs_ref[h]
            u = _gelu(p_ref[:, cols].astype(F32))
            z = p_ref[:, 2 * e + h * hd:2 * e + (h + 1) * hd].astype(F32)
            y_ref[:, cols] = (u * sv * (z * jax.nn.sigmoid(z))).astype(BF16)

    vec = pl.BlockSpec((1, e), lambda i: (0, 0))
    outs, job_outs = _call(
        name, body, grid=(t // CHUNK,),
        in_specs=[pl.BlockSpec((CHUNK, e3), lambda i: (i, 0)), vec, vec,
                  pl.BlockSpec((HEADS, CHUNK, CHUNK), lambda i: (0, 0, 0)),
                  pl.BlockSpec((HEADS, CHUNK, 1), lambda i: (0, 0, 0))],
        out_specs=[pl.BlockSpec((CHUNK, e), lambda i: (i, 0))],
        out_shape=[jax.ShapeDtypeStruct((t, e), BF16)],
        scratch=[pltpu.VMEM((CHUNK, e), F32)],
        sem=("arbitrary",) if jobs else ("parallel",),
        args=[proj, ln_g, ln_b, w_tril, bias_t], jobs=jobs)
    return outs[0], job_outs


def _gate_a_bwd(name, proj, dy, ln_g, ln_b, w_tril, w_tril_t, bias_t, jobs=()):
    t, e3 = proj.shape
    e = e3 // 3
    hd = e // HEADS

    def body(p_ref, dy_ref, g_ref, b_ref, w_ref, wt_ref, bs_ref,
             dp_ref, dw_ref, dbs_ref, dg_ref, db_ref, vh_ref, dvn_ref):
        @pl.when(pl.program_id(0) == 0)
        def _():
            dw_ref[...] = jnp.zeros_like(dw_ref)
            dbs_ref[...] = jnp.zeros_like(dbs_ref)
            dg_ref[...] = jnp.zeros_like(dg_ref)
            db_ref[...] = jnp.zeros_like(db_ref)

        s1 = jnp.zeros((CHUNK, 1), F32)
        s2 = jnp.zeros((CHUNK, 1), F32)
        for h in range(HEADS):
            cols = slice(h * hd, (h + 1) * hd)
            vg = _gelu(p_ref[:, e + h * hd:e + (h + 1) * hd].astype(F32))
            vh_ref[:, cols] = vg
            s1 += jnp.sum(vg, axis=1, keepdims=True)
            s2 += jnp.sum(vg * vg, axis=1, keepdims=True)
        mu = s1 * (1.0 / e)
        rstd = lax.rsqrt(s2 * (1.0 / e) - mu * mu + NORM_EPS)
        m1 = jnp.zeros((CHUNK, 1), F32)
        m2 = jnp.zeros((CHUNK, 1), F32)
        for h in range(HEADS):
            cols = slice(h * hd, (h + 1) * hd)
            vhat = (vh_ref[:, cols] - mu) * rstd
            vh_ref[:, cols] = vhat
            g = g_ref[:, cols]
            vn = (vhat * g + b_ref[:, cols]).astype(BF16)
            sv = jnp.dot(w_ref[h], vn, preferred_element_type=F32) + bs_ref[h]
            u_pre = p_ref[:, cols].astype(F32)
            u = _gelu(u_pre)
            z = p_ref[:, 2 * e + h * hd:2 * e + (h + 1) * hd].astype(F32)
            sig = jax.nn.sigmoid(z)
            dyv = dy_ref[:, cols].astype(F32)
            dsgu = dyv * (z * sig)
            dp_ref[:, 2 * e + h * hd:2 * e + (h + 1) * hd] = (dyv * (u * sv) * (sig * (1.0 + z * (1.0 - sig)))).astype(BF16)
            dp_ref[:, cols] = (dsgu * sv * _gelu_grad(u_pre)).astype(BF16)
            dsv = dsgu * u
            dsv_b = dsv.astype(BF16)
            dbs_ref[h] += jnp.sum(dsv, axis=1, keepdims=True)
            dw_ref[h] += lax.dot_general(dsv_b, vn, _DOT_DIMS["nt"], preferred_element_type=F32)
            dvn = jnp.dot(wt_ref[h], dsv_b, preferred_element_type=F32)
            dvn_ref[:, cols] = dvn
            dg_ref[:, cols] += jnp.sum(dvn * vhat, axis=0, keepdims=True)
            db_ref[:, cols] += jnp.sum(dvn, axis=0, keepdims=True)
            dvhat = dvn * g
            m1 += jnp.sum(dvhat, axis=1, keepdims=True)
            m2 += jnp.sum(dvhat * vhat, axis=1, keepdims=True)
        m1 = m1 * (1.0 / e)
        m2 = m2 * (1.0 / e)
        for h in range(HEADS):
            cols = slice(h * hd, (h + 1) * hd)
            dvg = rstd * (dvn_ref[:, cols] * g_ref[:, cols] - m1 - vh_ref[:, cols] * m2)
            v_pre = p_ref[:, e + h * hd:e + (h + 1) * hd].astype(F32)
            dp_ref[:, e + h * hd:e + (h + 1) * hd] = (dvg * _gelu_grad(v_pre)).astype(BF16)

    vec = pl.BlockSpec((1, e), lambda i: (0, 0))
    mats = pl.BlockSpec((HEADS, CHUNK, CHUNK), lambda i: (0, 0, 0))
    cols1 = pl.BlockSpec((HEADS, CHUNK, 1), lambda i: (0, 0, 0))
    return _call(
        name, body, grid=(t // CHUNK,),
        in_specs=[pl.BlockSpec((CHUNK, e3), lambda i: (i, 0)), pl.BlockSpec((CHUNK, e), lambda i: (i, 0)),
                  vec, vec, mats, mats, cols1],
        out_specs=[pl.BlockSpec((CHUNK, e3), lambda i: (i, 0)), mats, cols1, vec, vec],
        out_shape=[jax.ShapeDtypeStruct((t, e3), BF16), jax.ShapeDtypeStruct((HEADS, CHUNK, CHUNK), F32),
                   jax.ShapeDtypeStruct((HEADS, CHUNK, 1), F32), jax.ShapeDtypeStruct((1, e), F32),
                   jax.ShapeDtypeStruct((1, e), F32)],
        scratch=[pltpu.VMEM((CHUNK, e), F32), pltpu.VMEM((CHUNK, e), F32)],
        sem=("arbitrary",), args=[proj, dy, ln_g, ln_b, w_tril, w_tril_t, bias_t], jobs=jobs)


def _window_sum(xx, steps, backward):
    n = xx.shape[0]
    span = 1
    for _ in range(steps):
        xx = xx + pltpu.roll(xx, (n - span) if backward else span, axis=0)
        span *= 2
    return xx


def _pool_fwd(name, proj, e, seq, ts, tc):
    t = proj.shape[0]
    gw = e // len(WINDOWS)
    per_group = gw // tc
    hb = ts // HALO

    def body(p_ref, halo_ref, o_ref):
        i = pl.program_id(0)
        grp = pl.program_id(1) // per_group
        pos0 = (i * ts) % seq
        x = p_ref[...].astype(F32)
        halo = halo_ref[...].astype(F32) * jnp.where(pos0 == 0, 0.0, 1.0)
        xx = jnp.concatenate([halo, x], axis=0)
        pos = (pos0 + lax.broadcasted_iota(jnp.int32, (ts, 1), 0)).astype(F32)
        for k, w in enumerate(WINDOWS):
            @pl.when(grp == k)
            def _(k=k, w=w):
                win = _window_sum(xx, k + 1, False)[HALO:, :]
                o_ref[...] = (win / jnp.minimum(pos + 1.0, float(w)) - x).astype(BF16)

    return pl.pallas_call(
        body, name=name, grid=(t // ts, e // tc),
        in_specs=[pl.BlockSpec((ts, tc), lambda i, j: (i, j)),
                  pl.BlockSpec((HALO, tc), lambda i, j: (jnp.maximum(i * hb - 1, 0), j))],
        out_specs=pl.BlockSpec((ts, tc), lambda i, j: (i, j)),
        out_shape=jax.ShapeDtypeStruct((t, e), BF16),
        compiler_params=_params("parallel", "parallel"))(proj, proj)


def _pool_bwd(name, dpooled, dproj, seq, ts, tc):
    t, e = dpooled.shape
    gw = e // len(WINDOWS)
    per_group = gw // tc
    hb = ts // HALO
    last = t // HALO - 1

    def body(d_ref, halo_ref, alias_ref, o_ref):
        del alias_ref
        i = pl.program_id(0)
        grp = pl.program_id(1) // per_group
        pos0 = (i * ts) % seq
        d = d_ref[...]
        pos = (pos0 + lax.broadcasted_iota(jnp.int32, (ts, 1), 0)).astype(F32)
        at_end = ((i + 1) * ts) % seq == 0
        for k, w in enumerate(WINDOWS):
            @pl.when(grp == k)
            def _(k=k, w=w):
                r = d / jnp.minimum(pos + 1.0, float(w))
                rh = halo_ref[...] * jnp.where(at_end, 0.0, 1.0 / w)
                win = _window_sum(jnp.concatenate([r, rh], axis=0), k + 1, True)[:ts, :]
                o_ref[...] = (win - d).astype(BF16)

    return pl.pallas_call(
        body, name=name, grid=(t // ts, e // tc),
        in_specs=[pl.BlockSpec((ts, tc), lambda i, j: (i, j)),
                  pl.BlockSpec((HALO, tc), lambda i, j: (jnp.minimum((i + 1) * hb, last), j)),
                  ANY],
        out_specs=pl.BlockSpec((ts, tc), lambda i, j: (i, j)),
        out_shape=jax.ShapeDtypeStruct(dproj.shape, BF16),
        input_output_aliases={2: 0},
        compiler_params=_params("parallel", "parallel"))(dpooled, dpooled, dproj)


def _gate_b_fwd(name, q, proj, b_grp, scale, tr, tc):
    t, e = q.shape
    zoff = e // tc

    def body(q_ref, z_ref, b_ref, s_ref, y_ref):
        z = z_ref[...].astype(F32)
        mixed = (q_ref[...].astype(F32) + b_ref[...]) * s_ref[...]
        y_ref[...] = (mixed * (z * jax.nn.sigmoid(z))).astype(BF16)

    blk = pl.BlockSpec((tr, tc), lambda i, j: (i, j))
    vec = pl.BlockSpec((1, tc), lambda i, j: (0, j))
    return pl.pallas_call(
        body, name=name, grid=(t // tr, e // tc),
        in_specs=[blk, pl.BlockSpec((tr, tc), lambda i, j: (i, zoff + j)), vec, vec],
        out_specs=blk, out_shape=jax.ShapeDtypeStruct((t, e), BF16),
        compiler_params=_params("parallel", "parallel"))(q, proj, b_grp, scale)


def _gate_b_bwd(name, dy, q, proj, b_grp, scale, tr, tc):
    t, e = q.shape
    zoff = e // tc

    def body(dy_ref, q_ref, z_ref, b_ref, s_ref, dq_ref, dz_ref, ds_ref, db_ref):
        z = z_ref[...].astype(F32)
        sig = jax.nn.sigmoid(z)
        dyv = dy_ref[...].astype(F32)
        qb = q_ref[...].astype(F32) + b_ref[...]
        sc = s_ref[...]
        dmixed = dyv * (z * sig)
        dq = dmixed * sc
        dq_ref[...] = dq.astype(BF16)
        dz_ref[...] = (dyv * (qb * sc) * (sig * (1.0 + z * (1.0 - sig)))).astype(BF16)

        @pl.when(pl.program_id(1) == 0)
        def _():
            ds_ref[...] = jnp.zeros_like(ds_ref)
            db_ref[...] = jnp.zeros_like(db_ref)

        ds_ref[...] += jnp.sum(dmixed * qb, axis=0, keepdims=True)
        db_ref[...] += jnp.sum(dq, axis=0, keepdims=True)

    blk = pl.BlockSpec((tr, tc), lambda j, i: (i, j))
    zblk = pl.BlockSpec((tr, tc), lambda j, i: (i, zoff + j))
    vec = pl.BlockSpec((1, tc), lambda j, i: (0, j))
    return pl.pallas_call(
        body, name=name, grid=(e // tc, t // tr),
        in_specs=[blk, blk, zblk, vec, vec], out_specs=[blk, zblk, vec, vec],
        out_shape=[jax.ShapeDtypeStruct((t, e), BF16), jax.ShapeDtypeStruct((t, 2 * e), BF16),
                   jax.ShapeDtypeStruct((1, e), F32), jax.ShapeDtypeStruct((1, e), F32)],
        compiler_params=_params("parallel", "arbitrary"))(dy, q, proj, b_grp, scale)


def _place():
    return lax.axis_index("x"), lax.axis_index("y"), lax.axis_index("c")


def _remote(src, dst, sems, to):
    return pltpu.make_async_remote_copy(src_ref=src, dst_ref=dst, send_sem=sems[0], recv_sem=sems[1],
                                        device_id=to, device_id_type=MESH_ID)


GATHER_SEMS = 8


def _gather_job(arrays, part, starts=(0.0,)):
    n = len(arrays)
    fresh = part != "pass"

    def copies(ins, outs, sem, local_sem):
        x, y, c = _place()
        sibling, across_x, across_y = (x, y, 1 - c), (1 - x, y, c), (x, 1 - y, c)
        send, relay, passed = [], [], []
        for a in range(n):
            out = outs[a]
            if fresh:
                mine = out.at[4 * x + 2 * y + c]
                send += [_remote(ins[a], mine, sem(GATHER_SEMS * a + k), to)
                         for k, to in enumerate([sibling, across_x, across_y])]
                send.append(pltpu.make_async_copy(ins[a], mine, local_sem(a)))
                from_x = out.at[4 * (1 - x) + 2 * y + c]
                from_y = out.at[4 * x + 2 * (1 - y) + c]
                rows = out.shape[1]
                half = rows // 2 if rows % 2 == 0 else rows
                first = from_x.at[pl.ds(0, half)]
                relay.append(_remote(first, first, sem(GATHER_SEMS * a + 3), across_y))
                if half < rows:
                    second = from_y.at[pl.ds(half, rows - half)]
                    relay.append(_remote(second, second, sem(GATHER_SEMS * a + 4), across_x))
                relay += [_remote(from_x, from_x, sem(GATHER_SEMS * a + 5), sibling),
                          _remote(from_y, from_y, sem(GATHER_SEMS * a + 6), sibling)]
            if part != "send_relay":
                diagonal = out.at[4 * (1 - x) + 2 * (1 - y) + c]
                passed.append(_remote(diagonal, diagonal, sem(GATHER_SEMS * a + 7), sibling))
        return {"all": [send, relay, passed], "send_relay": [send, relay], "pass": [passed]}[part]

    shapes = [jax.ShapeDtypeStruct(((NDEV,) + b.shape) if fresh else b.shape, b.dtype) for b in arrays]
    return _Job(arrays, shapes, {} if fresh else {a: a for a in range(n)}, GATHER_SEMS * n, n, copies, starts)


def _comm_call(name, jobs):
    _, job_outs = _call(name, lambda: None, grid=(1,), in_specs=[], out_specs=[], out_shape=[], args=[],
                        sem=("arbitrary",), jobs=jobs)
    return job_outs


def _all_gather(name, blocks):
    return _comm_call(name, [_gather_job(blocks, "all", (0.0, 0.0, 0.0))])


def _to_sibling_job(grads, by_class):
    n = len(grads)

    def copies(ins, outs, sem, local_sem):
        del local_sem
        x, y, c = _place()
        return [[_remote(ins[a].at[q if by_class[a] else 2 * q + 1 - c], outs[a].at[q], sem(4 * a + q), (x, y, 1 - c))
                 for a in range(n) for q in range(4)]]

    return _Job(grads, [jax.ShapeDtypeStruct((4,) + g.shape[1:], g.dtype) for g in grads], {}, 4 * n, 0, copies)


def _to_chips_job(sums, rows=None, into=None):
    n = len(sums)

    def copies(ins, outs, sem, local_sem):
        del local_sem
        x, y, c = _place()
        chips = [(1 - x, y, c), (x, 1 - y, c), (1 - x, 1 - y, c)]

        def part(ref, j):
            return ref.at[j] if rows is None else ref.at[j, pl.ds(rows[0], rows[1])]

        return [[_remote(part(ins[a], j), part(outs[a], j), sem(3 * a + j), chips[j])
                 for a in range(n) for j in range(3)]]

    shapes = [jax.ShapeDtypeStruct(s.shape, s.dtype) for s in sums]
    if into is None:
        return _Job(sums, shapes, {}, 3 * n, 0, copies)
    return _Job(list(sums) + list(into), shapes, {n + a: a for a in range(n)}, 3 * n, 0, copies)


def _adamw(w, g, m, v):
    m = ADAM_B1 * m + (1.0 - ADAM_B1) * g
    v = ADAM_B2 * v + (1.0 - ADAM_B2) * (g * g)
    m_hat = m / (1.0 - ADAM_B1 ** ADAM_STEP)
    v_hat = v / (1.0 - ADAM_B2 ** ADAM_STEP)
    delta = -ADAM_LR * (m_hat / (jnp.sqrt(v_hat) + ADAM_EPS) + ADAM_WD * w)
    return delta, m, v


def _row_tile(rows, cols):
    return _tile(rows, max(F32_SUBLANES, (3 * 128 * 1024) // cols), F32_SUBLANES)


def _chip_sums(name, ids, grad, from_sibling, by_class=False):
    _, r, c = grad.shape
    tr = _row_tile(r, c)

    def body(ids_ref, g_ref, s_ref, o_ref):
        del ids_ref
        o_ref[...] = (g_ref[...].astype(F32) + s_ref[...].astype(F32)).astype(o_ref.dtype)

    return pl.pallas_call(
        body, name=name,
        grid_spec=pltpu.PrefetchScalarGridSpec(
            num_scalar_prefetch=1, grid=(3, r // tr),
            in_specs=[pl.BlockSpec((None, tr, c), (lambda j, i, ids: (ids[j], i, 0)) if by_class
                                   else (lambda j, i, ids: (2 * ids[j] + ids[3], i, 0))),
                      pl.BlockSpec((None, tr, c), lambda j, i, ids: (ids[j], i, 0))],
            out_specs=pl.BlockSpec((None, tr, c), lambda j, i, ids: (j, i, 0))),
        out_shape=jax.ShapeDtypeStruct((3, r, c), grad.dtype),
        compiler_params=_params("parallel", "parallel"))(ids, grad, from_sibling)


def _reduce_adamw(name, ids, grad, from_sibling, from_chips, w, m, v, by_class=False):
    _, r, c = grad.shape
    tr = _row_tile(r, c)

    def body(ids_ref, g_ref, s_ref, c_ref, w_ref, m_ref, v_ref, og_ref, od_ref, om_ref, ov_ref):
        del ids_ref
        g = g_ref[...].astype(F32) + s_ref[...].astype(F32)
        for j in range(3):
            g = g + c_ref[j].astype(F32)
        og_ref[...] = g
        od_ref[...], om_ref[...], ov_ref[...] = _adamw(w_ref[...], g, m_ref[...], v_ref[...])

    blk = pl.BlockSpec((tr, c), lambda i, ids: (i, 0))
    out = jax.ShapeDtypeStruct((r, c), F32)
    return pl.pallas_call(
        body, name=name,
        grid_spec=pltpu.PrefetchScalarGridSpec(
            num_scalar_prefetch=1, grid=(r // tr,),
            in_specs=[pl.BlockSpec((None, tr, c), lambda i, ids: (ids[4 if by_class else 5], i, 0)),
                      pl.BlockSpec((None, tr, c), lambda i, ids: (ids[4], i, 0)),
                      pl.BlockSpec((3, tr, c), lambda i, ids: (0, i, 0)), blk, blk, blk],
            out_specs=[blk, blk, blk, blk]),
        out_shape=[out, out, out, out],
        compiler_params=_params("parallel"))(ids, grad, from_sibling, from_chips, w, m, v)


def _small_adamw(name, gathered, mask, w, m, v):
    _, r, c = gathered.shape

    def body(g_ref, k_ref, w_ref, m_ref, v_ref, og_ref, od_ref, om_ref, ov_ref):
        g = g_ref[0]
        for dev in range(1, NDEV):
            g = g + g_ref[dev]
        g = g * k_ref[...]
        og_ref[...] = g
        od_ref[...], om_ref[...], ov_ref[...] = _adamw(w_ref[...], g, m_ref[...], v_ref[...])

    out = jax.ShapeDtypeStruct((r, c), F32)
    return pl.pallas_call(
        body, name=name, out_shape=[out, out, out, out],
        compiler_params=pltpu.CompilerParams(vmem_limit_bytes=VMEM_LIMIT))(gathered, mask, w, m, v)


def kernel(x, pre_norm, post_norm, a_w_in, a_ln_g, a_ln_b, a_w_s, a_b_s, a_w_out, b_w_in, b_w_grp, b_b_grp, b_scale, b_w_out, loss_target, m_pre_norm, m_post_norm, m_a_w_in, m_a_ln_g, m_a_ln_b, m_a_w_s, m_a_b_s, m_a_w_out, m_b_w_in, m_b_w_grp, m_b_b_grp, m_b_scale, m_b_w_out, v_pre_norm, v_post_norm, v_a_w_in, v_a_ln_g, v_a_ln_b, v_a_w_s, v_a_b_s, v_a_w_out, v_b_w_in, v_b_w_grp, v_b_b_grp, v_b_scale, v_b_w_out):
    bl, seq, d = x.shape
    t = bl * seq
    e = a_w_out.shape[1] * NDEV
    groups = len(WINDOWS)
    gw = e // groups
    c3, c2 = a_w_in.shape[2], b_w_in.shape[2]
    assert e // HEADS % LANES == 0 and seq % CHUNK == 0 and (2 * d // LANES) % F32_SUBLANES == 0

    bm = _tile(t, 1024)
    bn = 1024
    bd = _tile(d, 1024)
    be = _tile(e, 1024)
    bgw = _tile(gw, 1024)
    bk = _tile(e, 4096)
    tr = _tile(t, 128, F32_SUBLANES)
    ts = _tile(seq, 512, HALO)
    tc = _tile(gw, 1024)
    tre = _tile(t, 256, HALO)

    xc, yc, cc = _place()
    my_chip = 2 * xc + yc
    ids = jnp.stack([2 * (1 - xc) + yc, 2 * xc + (1 - yc), 2 * (1 - xc) + (1 - yc), cc, my_chip,
                     2 * my_chip + cc, 1 - cc, 0]).astype(jnp.int32)
    results = {}

    def finish(wname, g, from_sibling, from_chips, w, m, v, by_class=False):
        shape2d = g.shape[1:]
        outs = _reduce_adamw("adamw_" + wname, ids, g, from_sibling, from_chips, w.reshape(shape2d),
                             m.reshape(shape2d), v.reshape(shape2d), by_class=by_class)
        results[wname] = [o.reshape(w.shape) for o in outs]

    (wa_in,) = _all_gather("gather_a_in", [a_w_in[0].astype(BF16)])
    w_tril = jnp.tril(a_w_s[0]).astype(BF16)
    w_tril_t = jnp.swapaxes(w_tril, 1, 2)
    bias_t = a_b_s[0][:, :, None]
    x2d = x.reshape(t, d)
    tgt = loss_target.reshape(t, d)

    h0 = _rms_fwd("pre_norm_a", x2d, pre_norm[0:1], tr)
    proj_a, sent = _mm_in_proj("a_in_proj", h0, wa_in, bm, bn, jobs=[
        _gather_job([a_w_out[0].astype(BF16), b_w_in[0].astype(BF16)], "send_relay", (0.0, 0.62))])
    y_a, (wa_out, wb_in) = _gate_a_fwd("a_gate", proj_a, a_ln_g, a_ln_b, w_tril, bias_t, jobs=[_gather_job(sent, "pass")])
    wa_out = wa_out.reshape(e, d)
    m_a, sent = _mm_out_proj("a_out_proj", y_a, wa_out, bm, bd, bk, jobs=[
        _gather_job([b_w_grp[0].astype(BF16), b_w_out[0].astype(BF16), b_b_grp[0], b_scale], "send_relay", (0.0, 0.66))])
    x1, h1 = _post_pre("post_a_pre_b", x2d, m_a, post_norm[0:1], pre_norm[1:2], tr)
    proj_b, (wb_grp, wb_out, bgrp_g, scale_g) = _mm_in_proj("b_in_proj", h1, wb_in, bm, bn, jobs=[_gather_job(sent, "pass")])
    wb_out = wb_out.reshape(e, d)
    wb_grp = jnp.transpose(wb_grp, (1, 0, 2, 3)).reshape(groups, gw, gw)
    bgrp = jnp.transpose(bgrp_g, (1, 0, 2)).reshape(1, e)
    scale = scale_g.reshape(1, e)
    pooled = _pool_fwd("b_pool", proj_b, e, seq, ts, tc)
    q, _ = _mm_grouped("b_group_proj", pooled, wb_grp, "nn", bm, bgw, BF16)
    y_b = _gate_b_fwd("b_gate", q, proj_b, bgrp, scale, tre, tc)
    m_b, _ = _mm_out_proj("b_out_proj", y_b, wb_out, bm, bd, bk)
    sq, dres, dm_b, dg_post1 = _loss_head("loss_head", x1, m_b, tgt, post_norm[1:2], tr)
    loss = lax.psum(0.5 * jnp.sum(sq) / d, ("x", "y", "c"))

    dwb_out, _ = _mm_dw("b_out_dw", y_b, dm_b, be, bd)
    g_b_out = dwb_out.reshape(NDEV, e // NDEV, d)
    dy_b, (fs_b_out,) = _mm_dact_out("b_out_dact", dm_b, wb_out, bm, be, jobs=[_to_sibling_job([g_b_out], [False])])
    cs_b_out = _chip_sums("chip_sum_b_w_out", ids, g_b_out, fs_b_out)
    dq, dproj_b, dscale, dbgrp = _gate_b_bwd("b_gate_bwd", dy_b, q, proj_b, bgrp, scale, tre, tc)
    dwb_grp, _ = _mm_dw_grouped("b_group_dw", pooled, dq, groups, bgw, bgw)
    grp_names = ["b_w_grp", "b_b_grp", "b_scale"]
    g_grp = [jnp.transpose(dwb_grp.reshape(groups, NDEV, gw // NDEV, gw), (1, 0, 2, 3)).reshape(NDEV, groups * gw // NDEV, gw),
             jnp.transpose(dbgrp.reshape(groups, NDEV, gw // NDEV), (1, 0, 2)),
             dscale.reshape(NDEV, 1, e // NDEV)]
    dpooled, fs_grp = _mm_grouped("b_group_dact", dq, wb_grp, "nt", bm, bgw, F32, jobs=[_to_sibling_job(g_grp, [False] * 3)])
    cs_grp = [_chip_sums("chip_sum_" + n, ids, g, fs) for n, g, fs in zip(grp_names, g_grp, fs_grp)]
    dproj_b = _pool_bwd("b_pool_bwd", dpooled, dproj_b, seq, ts, tc)
    g_b_in, fc = _mm_dw_sharded("b_in_dw", h1, dproj_b, c2, bd, bn, jobs=[_to_chips_job([cs_b_out] + cs_grp)])
    finish("b_w_out", g_b_out, fs_b_out, fc[0], b_w_out, m_b_w_out, v_b_w_out)
    finish("b_w_grp", g_grp[0], fs_grp[0], fc[1], b_w_grp, m_b_w_grp, v_b_w_grp)
    finish("b_b_grp", g_grp[1], fs_grp[1], fc[2], b_b_grp, m_b_b_grp, v_b_b_grp)
    finish("b_scale", g_grp[2], fs_grp[2], fc[3], b_scale, m_b_scale, v_b_scale)
    dh1, (fs_b_in,) = _mm_dact_in("b_in_dact", dproj_b, wb_in, bm, bd, jobs=[_to_sibling_job([g_b_in], [False])])
    cs_b_in = _chip_sums("chip_sum_b_w_in", ids, g_b_in, fs_b_in)
    dres, dm_a, dg_pre1, dg_post0 = _norms_bwd_mid("norms_bwd_mid", dres, dh1, x1, pre_norm[1:2], m_a, post_norm[0:1], tr)

    dwa_out, fc_b_in = _mm_dw("a_out_dw", y_a, dm_a, be, bd, jobs=[_to_chips_job([cs_b_in], rows=(0, d // 2))])
    g_a_out = dwa_out.reshape(NDEV, e // NDEV, d)
    dy_a, (fs_a_out, fc_b_in) = _mm_dact_out("a_out_dact", dm_a, wa_out, bm, be, jobs=[
        _to_sibling_job([g_a_out], [False]), _to_chips_job([cs_b_in], rows=(d // 2, d - d // 2), into=fc_b_in)])
    finish("b_w_in", g_b_in, fs_b_in, fc_b_in, b_w_in, m_b_w_in, v_b_w_in)
    cs_a_out = _chip_sums("chip_sum_a_w_out", ids, g_a_out, fs_a_out)
    (dproj_a, dws, dbs, dlng, dlnb), (fc_a_out,) = _gate_a_bwd(
        "a_gate_bwd", proj_a, dy_a, a_ln_g, a_ln_b, w_tril, w_tril_t, bias_t, jobs=[_to_chips_job([cs_a_out])])
    finish("a_w_out", g_a_out, fs_a_out, fc_a_out, a_w_out, m_a_w_out, v_a_w_out)
    g_a_in_sib, _ = _mm_dw_class("a_in_dw_sibling", ids, 6, h0, dproj_a, c3, bd, bn)
    g_a_in, (fs_a_in,) = _mm_dw_class("a_in_dw_own", ids, 3, h0, dproj_a, c3, bd, bn, jobs=[_to_sibling_job([g_a_in_sib], [True])])
    cs_a_in = _chip_sums("chip_sum_a_w_in", ids, g_a_in, fs_a_in, by_class=True)
    dh0, (fc_a_in,) = _mm_dact_in("a_in_dact", dproj_a, wa_in, bm, bd, jobs=[_to_chips_job([cs_a_in])])
    finish("a_w_in", g_a_in, fs_a_in, fc_a_in, a_w_in, m_a_w_in, v_a_w_in, by_class=True)
    grad_x, dg_pre0 = _norm_bwd_first("norm_bwd_first", dres, dh0, x2d, pre_norm[0:1], tr)

    small = [
        ("pre_norm", jnp.concatenate([dg_pre0, dg_pre1], axis=0), pre_norm, m_pre_norm, v_pre_norm),
        ("post_norm", jnp.concatenate([dg_post0, dg_post1], axis=0), post_norm, m_post_norm, v_post_norm),
        ("a_ln_g", dlng, a_ln_g, m_a_ln_g, v_a_ln_g),
        ("a_ln_b", dlnb, a_ln_b, m_a_ln_b, v_a_ln_b),
        ("a_w_s", dws, a_w_s, m_a_w_s, v_a_w_s),
        ("a_b_s", dbs, a_b_s, m_a_b_s, v_a_b_s),
    ]

    def pack(arrays):
        return jnp.concatenate([a.reshape(-1, LANES) for a in arrays], axis=0)

    tril_mask = jnp.broadcast_to(jnp.tril(jnp.ones((CHUNK, CHUNK), F32)), a_w_s.shape)
    mask = pack([tril_mask if s[0] == "a_w_s" else jnp.ones(s[2].shape, F32) for s in small])
    (gathered,) = _all_gather("gather_small_grads", [pack([s[1] for s in small])])
    packed = _small_adamw("adamw_small", gathered, mask, pack([s[2] for s in small]),
                          pack([s[3] for s in small]), pack([s[4] for s in small]))
    row = 0
    for wname, _, w, _, _ in small:
        rows = w.size // LANES
        results[wname] = [p[row:row + rows].reshape(w.shape) for p in packed]
        row += rows

    order = ["pre_norm", "post_norm", "a_w_in", "a_ln_g", "a_ln_b", "a_w_s", "a_b_s", "a_w_out",
             "b_w_in", "b_w_grp", "b_b_grp", "b_scale", "b_w_out"]
    out = [loss, grad_x.reshape(x.shape)]
    for kind in range(4):
        out += [results[wname][kind] for wname in order]
    return tuple(out)
```
